```python
import jax, jax.numpy as jnp
from jax import lax
import numpy as np

D_MODEL = 2048
BATCH = 4
SEQ = 8192
DEPTH = 4

A_HEADS = 16
A_HEAD_DIM = 64
A_WIDTH = A_HEADS * A_HEAD_DIM
DECAY_LORA = 64
ICLR_LORA = 64
GATE_LORA = 128
A_COLS = 3 * A_WIDTH + DECAY_LORA + ICLR_LORA + GATE_LORA
B_HEADS = 8
B_KV_HEADS = 2
B_HEAD_DIM = 128
B_WIDTH = B_HEADS * B_HEAD_DIM
B_KV_WIDTH = B_KV_HEADS * B_HEAD_DIM
IDX_HEADS = 16
IDX_DIM = 64
MAX_TOPK = 256
Q_BLOCK = 128
B_COLS = B_WIDTH + 2 * B_KV_WIDTH + IDX_HEADS * IDX_DIM + IDX_DIM + IDX_HEADS
IN_COLS = A_COLS + B_COLS
MIX_WIDTH = A_WIDTH + B_WIDTH
CONV_WIDTH = 3
D_FF = 5632
N_EXPERTS = 8
TOP_K_EXPERTS = 2
EXPERT_FF = 5632
ROPE_THETA = 10000.0
NORM_EPS = 1e-6
GN_EPS = 64e-5
N_MOD = 6
N_EVEN = (DEPTH + 1) // 2
N_ODD = DEPTH // 2

kernel_name = 'hybrid_rwkv7_dsa_shortconv_moe_adaln'

F32 = jnp.float32


def rmsnorm_mod(x, g, shift, scale):
    xf = x.astype(F32)
    y = xf * lax.rsqrt(jnp.mean(xf * xf, axis=-1, keepdims=True) + NORM_EPS)
    y = y * g.astype(F32) * (1.0 + scale[:, None, :].astype(F32)) + shift[:, None, :].astype(F32)
    return y.astype(x.dtype)


def rope(t, pos):
    half = t.shape[-1] // 2
    inv = jnp.power(ROPE_THETA, -jnp.arange(half, dtype=F32) / half)
    ang = pos.astype(F32)[:, :, None, None] * inv
    cos, sin = jnp.cos(ang), jnp.sin(ang)
    t1 = t[..., :half].astype(F32)
    t2 = t[..., half:].astype(F32)
    return jnp.concatenate([t1 * cos - t2 * sin, t2 * cos + t1 * sin], axis=-1).astype(t.dtype)


def token_shift(z, mu):
    z_prev = jnp.pad(z, ((0, 0), (1, 0), (0, 0)))[:, :-1]
    return z + (z_prev - z) * mu


def rwkv7_time_mix(za, w0, w2, a0, a2, g2, k_k, k_a, r_k, lnx_g, lnx_b):
    bsz, seq, _ = za.shape
    o1 = 3 * A_WIDTH + DECAY_LORA
    r, k, v, xw, xa, xg = jnp.split(
        za, [A_WIDTH, 2 * A_WIDTH, 3 * A_WIDTH, o1, o1 + ICLR_LORA], axis=-1)
    wlog = -jax.nn.softplus(-(w0 + jnp.tanh(xw) @ w2).astype(F32)) - 0.5
    decay = jnp.exp(-jnp.exp(wlog))
    a = jax.nn.sigmoid((a0 + xa @ a2).astype(F32))
    g = jax.nn.sigmoid(xg) @ g2
    hs = (bsz, seq, A_HEADS, A_HEAD_DIM)
    kk = (k * k_k).astype(F32).reshape(hs)
    kk = kk / jnp.maximum(jnp.sqrt(jnp.sum(kk * kk, axis=-1, keepdims=True)), 1e-12)
    kf = k.astype(F32) * (1.0 + (a - 1.0) * k_a.astype(F32))
    rh = r.astype(F32).reshape(hs)
    kh = kf.reshape(hs)
    vh = v.astype(F32).reshape(hs)
    dh = decay.reshape(hs)
    ah = a.reshape(hs)

    def step(state, inp):
        r_t, d_t, k_t, v_t, av_t, bv_t = inp
        sa = jnp.einsum('bhvk,bhk->bhv', state, av_t)
        state = (state * d_t[:, :, None, :] + sa[..., None] * bv_t[:, :, None, :]
                 + v_t[..., None] * k_t[:, :, None, :])
        y = jnp.einsum('bhvk,bhk->bhv', state, r_t)
        return state, y

    xs = tuple(jnp.moveaxis(t, 1, 0) for t in (rh, dh, kh, vh, -kk, kk * ah))
    state0 = jnp.zeros((bsz, A_HEADS, A_HEAD_DIM, A_HEAD_DIM), F32)
    _, y = lax.scan(step, state0, xs)
    y = jnp.moveaxis(y, 0, 1)
    mu = jnp.mean(y, axis=-1, keepdims=True)
    var = jnp.mean(jnp.square(y - mu), axis=-1, keepdims=True)
    yn = ((y - mu) * lax.rsqrt(var + GN_EPS)).reshape(bsz, seq, A_WIDTH)
    yn = yn * lnx_g.astype(F32) + lnx_b.astype(F32)
    bonus = jnp.sum(rh * kh * r_k.astype(F32)[None, None], axis=-1, keepdims=True) * vh
    out = (yn + bonus.reshape(bsz, seq, A_WIDTH)) * g.astype(F32)
    return out.astype(za.dtype)


def dsa_attention(zb, positions):
    bsz, seq, _ = zb.shape
    o = np.cumsum([B_WIDTH, B_KV_WIDTH, B_KV_WIDTH, IDX_HEADS * IDX_DIM, IDX_DIM]).tolist()
    q, k, v, qi, ki, wi = jnp.split(zb, o, axis=-1)
    q = rope(q.reshape(bsz, seq, B_HEADS, B_HEAD_DIM), positions)
    k = rope(k.reshape(bsz, seq, B_KV_HEADS, B_HEAD_DIM), positions)
    v = v.reshape(bsz, seq, B_KV_HEADS, B_HEAD_DIM)
    qi = rope(qi.reshape(bsz, seq, IDX_HEADS, IDX_DIM), positions)
    ki = rope(ki.reshape(bsz, seq, 1, IDX_DIM), positions)[:, :, 0]
    wi = wi.astype(F32) * (IDX_HEADS ** -0.5 * IDX_DIM ** -0.5)
    topk = min(MAX_TOPK, seq // 4)
    nb = seq // Q_BLOCK
    group = B_HEADS // B_KV_HEADS
    key_idx = jnp.arange(seq, dtype=jnp.int32)
    gather = jax.vmap(lambda t, i: t[i])

    def blocks(t):
        return jnp.moveaxis(t.reshape(bsz, nb, Q_BLOCK, *t.shape[2:]), 1, 0)

    def one_block(args):
        qb, qib, wib, start = args
        q_idx = start + jnp.arange(Q_BLOCK, dtype=jnp.int32)
        logits = jnp.einsum('bqhd,bsd->bqhs', qib, ki, preferred_element_type=F32)
        score = jnp.einsum('bqhs,bqh->bqs', jax.nn.relu(logits), wib)
        causal = key_idx[None, :] <= q_idx[:, None]
        score = jnp.where(causal[None], score, -jnp.inf)
        top_val, top_idx = lax.top_k(score, topk)
        valid = jnp.isfinite(top_val)
        k_sel = gather(k, top_idx)
        v_sel = gather(v, top_idx)
        qg = qb.reshape(bsz, Q_BLOCK, B_KV_HEADS, group, B_HEAD_DIM)
        s = jnp.einsum('bqhgd,bqkhd->bqhgk', qg, k_sel, preferred_element_type=F32) * (B_HEAD_DIM ** -0.5)
        s = jnp.where(valid[:, :, None, None, :], s, -jnp.inf)
        p = jax.nn.softmax(s, axis=-1).astype(v.dtype)
        ob = jnp.einsum('bqhgk,bqkhd->bqhgd', p, v_sel)
        return ob.reshape(bsz, Q_BLOCK, B_WIDTH)

    starts = jnp.arange(nb, dtype=jnp.int32) * Q_BLOCK
    out = lax.map(one_block, (blocks(q), blocks(qi), blocks(wi), starts))
    return jnp.moveaxis(out, 0, 1).reshape(bsz, seq, B_WIDTH)


def parallel_mixer(h, positions, w_in, mu, w0, w2, a0, a2, g2, k_k, k_a, r_k, lnx_g, lnx_b, w_out):
    z = h @ w_in
    za = token_shift(z[..., :A_COLS], mu)
    zb = z[..., A_COLS:]
    ya = rwkv7_time_mix(za, w0, w2, a0, a2, g2, k_k, k_a, r_k, lnx_g, lnx_b)
    yb = dsa_attention(zb, positions)
    return jnp.concatenate([ya, yb], axis=-1) @ w_out


def short_conv_mixer(h, conv_in, conv_w, conv_out):
    bcx = h @ conv_in
    bg, cg, u = jnp.split(bcx, 3, axis=-1)
    conv = lax.conv_general_dilated(
        cg * u, conv_w[:, None, :], window_strides=(1,), padding=[(CONV_WIDTH - 1, 0)],
        dimension_numbers=('NWC', 'WIO', 'NWC'), feature_group_count=D_MODEL)
    return (bg * conv) @ conv_out


def swiglu(h, w1, w3, w2):
    return (jax.nn.silu(h @ w1) * (h @ w3)) @ w2


def moe_swiglu(h, router, w1, w3, w2):
    logits = (h @ router).astype(F32)
    top_val, top_idx = lax.top_k(logits, TOP_K_EXPERTS)
    gates = jax.nn.softmax(top_val, axis=-1)
    combine = jnp.sum(jax.nn.one_hot(top_idx, N_EXPERTS, dtype=F32) * gates[..., None], axis=-2)
    y = jnp.zeros(h.shape, F32)
    for e in range(N_EXPERTS):
        y = y + combine[..., e:e + 1] * swiglu(h, w1[e], w3[e], w2[e]).astype(F32)
    return y.astype(h.dtype)


def setup_inputs(seed: int = 0) -> dict:
    key = jax.random.key(seed)
    ks = iter(jax.random.split(key, 64))
    D = D_MODEL

    def nrm(shape, scale):
        return jax.random.normal(next(ks), shape, F32) * scale

    x = nrm((BATCH, SEQ, D), 1.0)
    c = nrm((BATCH, D), 1.0)
    offs = jax.random.randint(next(ks), (BATCH, 1), 0, 1024, dtype=jnp.int32)
    positions = offs + jnp.arange(SEQ, dtype=jnp.int32)[None, :]
    return {
        'x': x,
        'c': c,
        'positions': positions,
        'mod_w': nrm((D, N_MOD * D), 0.5 * D ** -0.5),
        'mod_b': nrm((N_MOD * D,), 0.02),
        'mod_table': nrm((DEPTH, N_MOD, D), 0.1),
        'final_table': nrm((2, D), 0.1),
        'norm_g': 1.0 + nrm((DEPTH, 2, D), 0.02),
        'final_g': 1.0 + nrm((D,), 0.02),
        'ev_w_in': nrm((N_EVEN, D, IN_COLS), D ** -0.5),
        'ev_mu': jax.random.uniform(next(ks), (N_EVEN, A_COLS), F32),
        'ev_w0': nrm((N_EVEN, A_WIDTH), 0.5),
        'ev_w2': nrm((N_EVEN, DECAY_LORA, A_WIDTH), 0.5 * DECAY_LORA ** -0.5),
        'ev_a0': nrm((N_EVEN, A_WIDTH), 0.5),
        'ev_a2': nrm((N_EVEN, ICLR_LORA, A_WIDTH), 0.5 * ICLR_LORA ** -0.5),
        'ev_g2': nrm((N_EVEN, GATE_LORA, A_WIDTH), GATE_LORA ** -0.5),
        'ev_k_k': 1.0 + nrm((N_EVEN, A_WIDTH), 0.1),
        'ev_k_a': 1.0 + nrm((N_EVEN, A_WIDTH), 0.1),
        'ev_r_k': nrm((N_EVEN, A_HEADS, A_HEAD_DIM), 0.1),
        'ev_lnx_g': 1.0 + nrm((N_EVEN, A_WIDTH), 0.02),
        'ev_lnx_b': nrm((N_EVEN, A_WIDTH), 0.02),
        'ev_w_out': nrm((N_EVEN, MIX_WIDTH, D), MIX_WIDTH ** -0.5),
        'ffn_w1': nrm((N_EVEN, D, D_FF), D ** -0.5),
        'ffn_w3': nrm((N_EVEN, D, D_FF), D ** -0.5),
        'ffn_w2': nrm((N_EVEN, D_FF, D), D_FF ** -0.5),
        'od_conv_in': nrm((N_ODD, D, 3 * D), D ** -0.5),
        'od_conv_w': nrm((N_ODD, CONV_WIDTH, D), CONV_WIDTH ** -0.5),
        'od_conv_out': nrm((N_ODD, D, D), D ** -0.5),
        'moe_router': nrm((N_ODD, D, N_EXPERTS), D ** -0.5),
        'moe_w1': nrm((N_ODD, N_EXPERTS, D, EXPERT_FF), D ** -0.5),
        'moe_w3': nrm((N_ODD, N_EXPERTS, D, EXPERT_FF), D ** -0.5),
        'moe_w2': nrm((N_ODD, N_EXPERTS, EXPERT_FF, D), EXPERT_FF ** -0.5),
    }


def reference(x, c, positions, mod_w, mod_b, mod_table, final_table, norm_g, final_g,
              ev_w_in, ev_mu, ev_w0, ev_w2, ev_a0, ev_a2, ev_g2, ev_k_k, ev_k_a, ev_r_k,
              ev_lnx_g, ev_lnx_b, ev_w_out, ffn_w1, ffn_w3, ffn_w2,
              od_conv_in, od_conv_w, od_conv_out, moe_router, moe_w1, moe_w3, moe_w2):
    bsz = x.shape[0]
    t0 = (jax.nn.silu(c) @ mod_w + mod_b).reshape(bsz, N_MOD, D_MODEL)
    for i in range(DEPTH):
        j = i // 2
        mod = t0 + mod_table[i][None]
        h = rmsnorm_mod(x, norm_g[i, 0], mod[:, 0], mod[:, 1])
        if i % 2 == 0:
            m = parallel_mixer(h, positions, ev_w_in[j], ev_mu[j], ev_w0[j], ev_w2[j], ev_a0[j],
                               ev_a2[j], ev_g2[j], ev_k_k[j], ev_k_a[j], ev_r_k[j],
                               ev_lnx_g[j], ev_lnx_b[j], ev_w_out[j])
        else:
            m = short_conv_mixer(h, od_conv_in[j], od_conv_w[j], od_conv_out[j])
        x = x + mod[:, 2][:, None, :] * m
        h = rmsnorm_mod(x, norm_g[i, 1], mod[:, 3], mod[:, 4])
        if i % 2 == 0:
            f = swiglu(h, ffn_w1[j], ffn_w3[j], ffn_w2[j])
        else:
            f = moe_swiglu(h, moe_router[j], moe_w1[j], moe_w3[j], moe_w2[j])
        x = x + mod[:, 5][:, None, :] * f
    return rmsnorm_mod(x, final_g, t0[:, 0] + final_table[0], t0[:, 1] + final_table[1])
```

```python
import functools

import jax
import jax.numpy as jnp
from jax import lax
from jax.experimental import pallas as pl
from jax.experimental.pallas import tpu as pltpu

F32 = jnp.float32
BF16 = jnp.bfloat16

D_MODEL = 2048
DEPTH = 4
A_HEADS = 16
A_HEAD_DIM = 64
A_WIDTH = A_HEADS * A_HEAD_DIM
DECAY_LORA = 64
ICLR_LORA = 64
GATE_LORA = 128
LORA_COLS = DECAY_LORA + ICLR_LORA + GATE_LORA
A_COLS = 3 * A_WIDTH + LORA_COLS
B_HEADS = 8
B_KV_HEADS = 2
B_HEAD_DIM = 128
B_WIDTH = B_HEADS * B_HEAD_DIM
B_KV_WIDTH = B_KV_HEADS * B_HEAD_DIM
IDX_HEADS = 16
IDX_DIM = 64
IDX_WIDTH = IDX_HEADS * IDX_DIM
MAX_TOPK = 256
Q_BLOCK = 128
CONV_WIDTH = 3
D_FF = 5632
N_EXPERTS = 8
ROPE_THETA = 10000.0
NORM_EPS = 1e-6
GN_EPS = 64e-5
N_MOD = 6

LANES = 128
SUBLANES = 8
VMEM_LIMIT = 56 << 20

ZC_R, ZC_K, ZC_V, ZC_Q, ZC_QI = 0, 1024, 2048, 3072, 4096
ZC_KB, ZC_VB, ZC_LORA, ZC_KW = 5120, 5376, 5632, 5888
Z_COLS = 6144


def _cparams(*sem):
    return pltpu.CompilerParams(dimension_semantics=sem, vmem_limit_bytes=VMEM_LIMIT)


NORM_ROWS = 256


def _norm_rows_into(x_ref, mul_ref, add_ref, h_ref):
    tm = x_ref.shape[0]

    def body(c, carry):
        r0 = pl.multiple_of(c * NORM_ROWS, NORM_ROWS)
        x = x_ref[pl.ds(r0, NORM_ROWS), :]
        ms = jnp.mean(x * x, axis=-1, keepdims=True)
        y = x * lax.rsqrt(ms + NORM_EPS)
        h_ref[pl.ds(r0, NORM_ROWS), :] = (y * mul_ref[...] + add_ref[...]).astype(h_ref.dtype)
        return carry

    lax.fori_loop(0, tm // NORM_ROWS, body, 0)


def _norm_mm_kernel(x_ref, mul_ref, add_ref, w_ref, o_ref, h_ref):
    @pl.when(pl.program_id(1) == 0)
    def _():
        _norm_rows_into(x_ref, mul_ref, add_ref, h_ref)

    o_ref[...] = jnp.dot(h_ref[...], w_ref[...], preferred_element_type=F32).astype(o_ref.dtype)


def norm_matmul(x, mul, add, w, seq, *, tm=1024, tn=512, out_dtype=F32):
    t, d = x.shape
    n = w.shape[1]
    tm = min(tm, seq)
    bpb = seq // tm
    return pl.pallas_call(
        _norm_mm_kernel,
        grid=(t // tm, n // tn),
        in_specs=[
            pl.BlockSpec((tm, d), lambda i, j: (i, 0)),
            pl.BlockSpec((None, 1, d), lambda i, j: (i // bpb, 0, 0)),
            pl.BlockSpec((None, 1, d), lambda i, j: (i // bpb, 0, 0)),
            pl.BlockSpec((d, tn), lambda i, j: (0, j)),
        ],
        out_specs=pl.BlockSpec((tm, tn), lambda i, j: (i, j)),
        out_shape=jax.ShapeDtypeStruct((t, n), out_dtype),
        scratch_shapes=[pltpu.VMEM((tm, d), BF16)],
        compiler_params=_cparams("parallel", "arbitrary"),
        name="norm_matmul",
    )(x, mul, add, w)


def _norm_swiglu_kernel(x_ref, mul_ref, add_ref, w1_ref, w3_ref, o_ref, h_ref):
    @pl.when(pl.program_id(1) == 0)
    def _():
        _norm_rows_into(x_ref, mul_ref, add_ref, h_ref)

    h = h_ref[...]
    u = jnp.dot(h, w1_ref[...], preferred_element_type=F32)
    g = jnp.dot(h, w3_ref[...], preferred_element_type=F32)
    o_ref[...] = (u * jax.nn.sigmoid(u) * g).astype(o_ref.dtype)


def norm_swiglu(x, mul, add, w1, w3, seq, *, tm=1024, tn=512):
    t, d = x.shape
    n = w1.shape[1]
    tm = min(tm, seq)
    bpb = seq // tm
    return pl.pallas_call(
        _norm_swiglu_kernel,
        grid=(t // tm, n // tn),
        in_specs=[
            pl.BlockSpec((tm, d), lambda i, j: (i, 0)),
            pl.BlockSpec((None, 1, d), lambda i, j: (i // bpb, 0, 0)),
            pl.BlockSpec((None, 1, d), lambda i, j: (i // bpb, 0, 0)),
            pl.BlockSpec((d, tn), lambda i, j: (0, j)),
            pl.BlockSpec((d, tn), lambda i, j: (0, j)),
        ],
        out_specs=pl.BlockSpec((tm, tn), lambda i, j: (i, j)),
        out_shape=jax.ShapeDtypeStruct((t, n), BF16),
        scratch_shapes=[pltpu.VMEM((tm, d), BF16)],
        compiler_params=_cparams("parallel", "arbitrary"),
        name="norm_swiglu",
    )(x, mul, add, w1, w3)


def _mm_res_kernel(a_ref, w_ref, base_ref, cs_ref, o_ref):
    acc = jnp.dot(a_ref[...], w_ref[...], preferred_element_type=F32)
    o_ref[...] = base_ref[...] + cs_ref[...] * acc


def _mm_res_rows_kernel(a_ref, w_ref, base_ref, cs_ref, rs_ref, o_ref):
    acc = jnp.dot(a_ref[...], w_ref[...], preferred_element_type=F32)
    o_ref[...] = base_ref[...] + (cs_ref[...] * rs_ref[...]) * acc


def matmul_residual(a, w, base, cs, seq, rs=None, *, tm=1024, tn=256):
    t, k = a.shape
    n = w.shape[1]
    tm = min(tm, seq)
    bpb = seq // tm
    in_specs = [
        pl.BlockSpec((tm, k), lambda i, j: (i, 0)),
        pl.BlockSpec((k, tn), lambda i, j: (0, j)),
        pl.BlockSpec((tm, tn), lambda i, j: (i, j)),
        pl.BlockSpec((None, 1, tn), lambda i, j: (i // bpb, 0, j)),
    ]
    args = [a, w, base, cs]
    body = _mm_res_kernel
    if rs is not None:
        in_specs.append(pl.BlockSpec((tm, 1), lambda i, j: (i, 0)))
        args.append(rs)
        body = _mm_res_rows_kernel
    return pl.pallas_call(
        body,
        grid=(t // tm, n // tn),
        in_specs=in_specs,
        out_specs=pl.BlockSpec((tm, tn), lambda i, j: (i, j)),
        out_shape=jax.ShapeDtypeStruct((t, n), F32),
        compiler_params=_cparams("parallel", "arbitrary"),
        name="matmul_residual",
    )(*args)


def _final_norm_kernel(x_ref, mul_ref, add_ref, o_ref):
    x = x_ref[...]
    ms = jnp.mean(x * x, axis=-1, keepdims=True)
    o_ref[...] = x * lax.rsqrt(ms + NORM_EPS) * mul_ref[...] + add_ref[...]


def final_norm(x, mul, add, seq, *, tm=256):
    t, d = x.shape
    bpb = seq // tm
    return pl.pallas_call(
        _final_norm_kernel,
        grid=(t // tm,),
        in_specs=[
            pl.BlockSpec((tm, d), lambda i: (i, 0)),
            pl.BlockSpec((None, 1, d), lambda i: (i // bpb, 0, 0)),
            pl.BlockSpec((None, 1, d), lambda i: (i // bpb, 0, 0)),
        ],
        out_specs=pl.BlockSpec((tm, d), lambda i: (i, 0)),
        out_shape=jax.ShapeDtypeStruct((t, d), F32),
        compiler_params=_cparams("parallel"),
        name="final_norm",
    )(x, mul, add)


def _shift_rows(y, prev8, k):
    rolled = pltpu.roll(y, k, axis=0)
    row = lax.broadcasted_iota(jnp.int32, y.shape, 0)
    for r in range(k):
        rolled = jnp.where(row == r, prev8[SUBLANES - k + r:SUBLANES - k + r + 1, :], rolled)
    return rolled


def _conv_kernel(bg_ref, cg_ref, u_ref, cgp_ref, up_ref, w_ref, o_ref, *, bpb):
    first = (pl.program_id(0) % bpb) == 0
    y = cg_ref[...] * u_ref[...]
    yp = jnp.where(first, 0.0, cgp_ref[...] * up_ref[...])
    w = w_ref[...]
    conv = w[2:3, :] * y + w[1:2, :] * _shift_rows(y, yp, 1) + w[0:1, :] * _shift_rows(y, yp, 2)
    o_ref[...] = (bg_ref[...] * conv).astype(o_ref.dtype)


def gated_short_conv(bcx, conv_w, seq, *, tm=512):
    t = bcx.shape[0]
    d = D_MODEL
    tm = min(tm, seq)
    bpb = seq // tm
    r8 = tm // SUBLANES
    prev = lambda c: (lambda i: (jnp.maximum(i * r8 - 1, 0), c))
    return pl.pallas_call(
        functools.partial(_conv_kernel, bpb=bpb),
        grid=(t // tm,),
        in_specs=[
            pl.BlockSpec((tm, d), lambda i: (i, 0)),
            pl.BlockSpec((tm, d), lambda i: (i, 1)),
            pl.BlockSpec((tm, d), lambda i: (i, 2)),
            pl.BlockSpec((SUBLANES, d), prev(1)),
            pl.BlockSpec((SUBLANES, d), prev(2)),
            pl.BlockSpec((SUBLANES, d), lambda i: (0, 0)),
        ],
        out_specs=pl.BlockSpec((tm, d), lambda i: (i, 0)),
        out_shape=jax.ShapeDtypeStruct((t, d), BF16),
        compiler_params=_cparams("parallel"),
        name="gated_short_conv",
    )(bcx, bcx, bcx, bcx, bcx, conv_w)


def _cond_kernel(c_ref, w_ref, b_ref, o_ref):
    c = c_ref[...]
    a = (c * jax.nn.sigmoid(c)).astype(BF16)
    o_ref[...] = jnp.dot(a, w_ref[...].astype(BF16), preferred_element_type=F32) + b_ref[...]


def cond_proj(c8, w, b, *, tn=1024):
    d, n = w.shape
    return pl.pallas_call(
        _cond_kernel,
        grid=(n // tn,),
        in_specs=[pl.BlockSpec((SUBLANES, d), lambda j: (0, 0)),
                  pl.BlockSpec((d, tn), lambda j: (0, j)),
                  pl.BlockSpec((1, tn), lambda j: (0, j))],
        out_specs=pl.BlockSpec((SUBLANES, tn), lambda j: (0, j)),
        out_shape=jax.ShapeDtypeStruct((SUBLANES, n), F32),
        compiler_params=_cparams("parallel"),
        name="cond_proj",
    )(c8, w, b)


def _router_kernel(x_ref, mul_ref, add_ref, wr_ref, o_ref, h_ref):
    _norm_rows_into(x_ref, mul_ref, add_ref, h_ref)
    lg = jnp.dot(h_ref[...], wr_ref[...], preferred_element_type=F32)
    lane = lax.broadcasted_iota(jnp.int32, lg.shape, 1)
    neg = -jnp.inf
    lg = jnp.where(lane < N_EXPERTS, lg, neg)
    m1 = jnp.max(lg, axis=-1, keepdims=True)
    i1 = jnp.min(jnp.where(lg == m1, lane, LANES), axis=-1, keepdims=True)
    lg2 = jnp.where(lane == i1, neg, lg)
    m2 = jnp.max(lg2, axis=-1, keepdims=True)
    i2 = jnp.min(jnp.where(lg2 == m2, lane, LANES), axis=-1, keepdims=True)
    e = jnp.exp(m2 - m1)
    den = 1.0 + e
    o_ref[...] = jnp.where(lane == i1, 1.0 / den, 0.0) + jnp.where(lane == i2, e / den, 0.0)


def route_top2(x, mul, add, wr, seq, *, tm=256):
    t, d = x.shape
    tm = min(tm, seq)
    bpb = seq // tm
    return pl.pallas_call(
        _router_kernel,
        grid=(t // tm,),
        in_specs=[
            pl.BlockSpec((tm, d), lambda i: (i, 0)),
            pl.BlockSpec((None, 1, d), lambda i: (i // bpb, 0, 0)),
            pl.BlockSpec((None, 1, d), lambda i: (i // bpb, 0, 0)),
            pl.BlockSpec((d, LANES), lambda i: (0, 0)),
        ],
        out_specs=pl.BlockSpec((tm, LANES), lambda i: (i, 0)),
        out_shape=jax.ShapeDtypeStruct((t, LANES), F32),
        scratch_shapes=[pltpu.VMEM((tm, d), BF16)],
        compiler_params=_cparams("parallel"),
        name="moe_router",
    )(x, mul, add, wr)


def _mm(x, y):
    return jnp.dot(x.astype(BF16), y.astype(BF16), preferred_element_type=F32)


def _mm_nt(x, y):
    return lax.dot_general(x.astype(BF16), y.astype(BF16), (((1,), (1,)), ((), ())),
                           preferred_element_type=F32)


def _mm_tn(x, y):
    return lax.dot_general(x.astype(BF16), y.astype(BF16), (((0,), (0,)), ((), ())),
                           preferred_element_type=F32)


def _mm_split(x, w):
    hi = x.astype(BF16)
    lo = (x - hi.astype(F32)).astype(BF16)
    return (jnp.dot(hi, w, preferred_element_type=F32)
            + jnp.dot(lo, w, preferred_element_type=F32))


def _rwkv_prep_kernel(r_ref, k_ref, v_ref, lo_ref, rp_ref, kp_ref, vp_ref, lop_ref,
                      mur_ref, muk_ref, muv_ref, mul_ref, w2_ref, a2_ref, g2_ref, w0_ref, a0_ref,
                      ro_ref, ko_ref, vo_ref, ldo_ref, ao_ref, go_ref, *, bpb):
    first = (pl.program_id(0) % bpb) == 0

    def shifted(z_ref, p_ref, mu_ref):
        z = z_ref[...]
        zp = _shift_rows(z, jnp.where(first, 0.0, p_ref[...]), 1)
        return z + (zp - z) * mu_ref[...]

    ro_ref[...] = shifted(r_ref, rp_ref, mur_ref)
    ko_ref[...] = shifted(k_ref, kp_ref, muk_ref)
    vo_ref[...] = shifted(v_ref, vp_ref, muv_ref)
    lora = shifted(lo_ref, lop_ref, mul_ref)
    wl = w0_ref[...] + _mm(jnp.tanh(lora), w2_ref[...])
    softplus = jnp.maximum(-wl, 0.0) + jnp.log(1.0 + jnp.exp(-jnp.abs(wl)))
    ldo_ref[...] = -jnp.exp(-softplus - 0.5)
    sg = jax.nn.sigmoid(lora)
    ao_ref[...] = jax.nn.sigmoid(a0_ref[...] + _mm(lora, a2_ref[...]))
    go_ref[...] = _mm(sg, g2_ref[...])


def rwkv_prep(z, mu_r, mu_k, mu_v, mu_l, w2p, a2p, g2p, w0, a0, seq, *, tm=256):
    t = z.shape[0]
    tm = min(tm, seq)
    bpb = seq // tm
    r8 = tm // SUBLANES
    aw, lw = A_WIDTH, LORA_COLS
    cur = lambda c, w: pl.BlockSpec((tm, w), lambda i: (i, c // w))
    prev = lambda c, w: pl.BlockSpec((SUBLANES, w), lambda i: (jnp.maximum(i * r8 - 1, 0), c // w))
    vec = lambda w: pl.BlockSpec((1, w), lambda i: (0, 0))
    full = lambda a, b: pl.BlockSpec((a, b), lambda i: (0, 0))
    out = pl.BlockSpec((tm, aw), lambda i: (i, 0))
    return pl.pallas_call(
        functools.partial(_rwkv_prep_kernel, bpb=bpb),
        grid=(t // tm,),
        in_specs=[cur(ZC_R, aw), cur(ZC_K, aw), cur(ZC_V, aw), cur(ZC_LORA, lw),
                  prev(ZC_R, aw), prev(ZC_K, aw), prev(ZC_V, aw), prev(ZC_LORA, lw),
                  vec(aw), vec(aw), vec(aw), vec(lw),
                  full(lw, aw), full(lw, aw), full(lw, aw), vec(aw), vec(aw)],
        out_specs=[out] * 6,
        out_shape=[jax.ShapeDtypeStruct((t, aw), F32)] * 6,
        compiler_params=_cparams("parallel"),
        name="rwkv_prep",
    )(z, z, z, z, z, z, z, z, mu_r, mu_k, mu_v, mu_l, w2p, a2p, g2p, w0, a0)


RWKV_CHUNK = 64
RWKV_PAIR = 2 * A_HEAD_DIM


def _rwkv_chunk_kernel(r_ref, k_ref, v_ref, ld_ref, a_ref, g_ref,
                       kkw_ref, kaw_ref, rkw_ref, lng_ref, lnb_ref, o_ref, s_ref, *, n_chunks):
    c_len, w = RWKV_CHUNK, RWKV_PAIR

    @pl.when(pl.program_id(2) == 0)
    def _():
        s_ref[...] = jnp.zeros_like(s_ref)

    row = lax.broadcasted_iota(jnp.int32, (w, w), 0)
    col = lax.broadcasted_iota(jnp.int32, (w, w), 1)
    same = (row >> 6) == (col >> 6)
    rt_, ct_ = row & (c_len - 1), col & (c_len - 1)
    strict = same & (ct_ < rt_)
    incl = same & (ct_ <= rt_)
    eye = jnp.where(row == col, 1.0, 0.0)
    ones_bd = jnp.where(same, 1.0, 0.0).astype(BF16)
    tr = lax.broadcasted_iota(jnp.int32, (c_len, c_len), 0)
    tc = lax.broadcasted_iota(jnp.int32, (c_len, c_len), 1)
    tri = jnp.where(tc <= tr, 1.0, 0.0).astype(BF16)
    head0 = lax.broadcasted_iota(jnp.int32, (c_len, w), 1) < A_HEAD_DIM

    def stack(x):
        return jnp.concatenate([jnp.where(head0, x, 0.0), jnp.where(head0, 0.0, x)], axis=0)

    def unstack(x):
        return x[:c_len] + x[c_len:]

    kkw, kaw, rkw = kkw_ref[...], kaw_ref[...], rkw_ref[...]
    lng, lnb = lng_ref[...], lnb_ref[...]
    inv_n = 1.0 / A_HEAD_DIM

    def chunk(c, carry):
        t0 = pl.multiple_of(c * c_len, c_len)
        sl = pl.ds(t0, c_len)
        rc, kc, vc, ld, ac = r_ref[sl, :], k_ref[sl, :], v_ref[sl, :], ld_ref[sl, :], a_ref[sl, :]
        kk = kc * kkw
        kkn = kk / jnp.maximum(jnp.sqrt(_mm_split(kk * kk, ones_bd)), 1e-12)
        kf = kc * (1.0 + (ac - 1.0) * kaw)
        ld_hi = ld.astype(BF16)
        ld_lo = (ld - ld_hi.astype(F32)).astype(BF16)
        logp = (jnp.dot(tri, ld_hi, preferred_element_type=F32)
                + jnp.dot(tri, ld_lo, preferred_element_type=F32))
        p = jnp.exp(logp)
        inv_p = jnp.exp(-logp)
        p_prev = jnp.exp(logp - ld)
        at_s = stack(-kkn * p_prev).astype(BF16)
        bt_s = stack(kkn * ac * inv_p).astype(BF16)
        kt_s = stack(kf * inv_p).astype(BF16)
        rt = stack(rc * p)
        rt_s = rt.astype(BF16)
        v_s = stack(vc).astype(BF16)

        a_ab = jnp.where(strict, _mm_nt(at_s, bt_s), 0.0)
        a_ak = jnp.where(strict, _mm_nt(at_s, kt_s), 0.0)
        y_ab = jnp.where(incl, _mm_nt(rt_s, bt_s), 0.0)
        y_ak = jnp.where(incl, _mm_nt(rt_s, kt_s), 0.0)
        tinv = eye + a_ab
        pw = a_ab
        n = 1
        while 2 * n < c_len:
            pw = _mm(pw, pw)
            tinv = tinv + _mm(tinv, pw)
            n *= 2
        w_ = _mm(tinv, at_s)
        u0 = _mm(tinv, _mm(a_ak, v_s))
        rq = rt + _mm(y_ab, w_)
        y0 = _mm(y_ab, u0) + _mm(y_ak, v_s)
        gmat = _mm_tn(w_, bt_s)
        npre = _mm_tn(u0, bt_s) + _mm_tn(v_s, kt_s)
        p_end = p[c_len - 1:c_len, :]

        s = s_ref[...]
        y = unstack(_mm_nt(rq, s) + y0)
        s_ref[...] = (s + _mm(s, gmat) + npre) * p_end

        mu = _mm_split(y, ones_bd) * inv_n
        yc = y - mu
        var = _mm_split(yc * yc, ones_bd) * inv_n
        yn = yc * lax.rsqrt(var + GN_EPS) * lng + lnb
        bonus = _mm_split(rc * kf * rkw, ones_bd) * vc
        o_ref[sl, :] = ((yn + bonus) * g_ref[sl, :]).astype(o_ref.dtype)
        return carry

    lax.fori_loop(0, n_chunks, chunk, 0)


def rwkv_mix(r, k, v, ld, a, g, kkw, kaw, rkw, lng, lnb, *, tl=256):
    b, s, aw = r.shape
    tl = min(tl, s)
    w = RWKV_PAIR
    tok = pl.BlockSpec((None, tl, w), lambda bi, hi, si: (bi, si, hi))
    par = pl.BlockSpec((1, w), lambda bi, hi, si: (0, hi))
    return pl.pallas_call(
        functools.partial(_rwkv_chunk_kernel, n_chunks=tl // RWKV_CHUNK),
        grid=(b, aw // w, s // tl),
        in_specs=[tok] * 6 + [par] * 5,
        out_specs=tok,
        out_shape=jax.ShapeDtypeStruct((b, s, aw), BF16),
        scratch_shapes=[pltpu.VMEM((w, w), F32)],
        compiler_params=_cparams("parallel", "parallel", "arbitrary"),
        name="rwkv_mix",
    )(r, k, v, ld, a, g, kkw, kaw, rkw, lng, lnb)


def rope_tables(positions):
    pos = positions.reshape(-1).astype(F32)[:, None]

    def cs(half):
        inv = jnp.power(ROPE_THETA, -jnp.arange(half, dtype=F32) / half)
        ang = pos * inv
        return jnp.cos(ang), jnp.sin(ang)

    c64, s64 = cs(B_HEAD_DIM // 2)
    c32, s32 = cs(IDX_DIM // 2)
    z32 = jnp.zeros_like(s32)
    cat = lambda *xs: jnp.concatenate(xs, axis=-1)
    return (cat(c64, c64), cat(-s64, s64), cat(c32, c32, c32, c32),
            cat(-s32, z32, -s32, z32), cat(z32, s32, z32, s32))


def _dsa_prep_kernel(q_ref, qi_ref, kb_ref, vb_ref, kw_ref, ca_ref, sa_ref, cb_ref, sb1_ref, sb2_ref,
                     qo_ref, qio_ref, ko_ref, vo_ref, kio_ref, wo_ref):
    ca, sa = ca_ref[...], sa_ref[...]
    cb, sb1, sb2 = cb_ref[...], sb1_ref[...], sb2_ref[...]

    def rope_a(x):
        return x * ca + pltpu.roll(x, LANES // 2, axis=1) * sa

    def rope_b(x):
        return (x * cb + pltpu.roll(x, LANES - IDX_DIM // 2, axis=1) * sb1
                + pltpu.roll(x, IDX_DIM // 2, axis=1) * sb2)

    for h in range(B_WIDTH // LANES):
        sl = slice(h * LANES, (h + 1) * LANES)
        qo_ref[:, sl] = rope_a(q_ref[:, sl]).astype(BF16)
    for h in range(B_KV_WIDTH // LANES):
        sl = slice(h * LANES, (h + 1) * LANES)
        ko_ref[:, sl] = rope_a(kb_ref[:, sl]).astype(BF16)
    vo_ref[...] = vb_ref[...].astype(BF16)
    for h in range(IDX_WIDTH // LANES):
        sl = slice(h * LANES, (h + 1) * LANES)
        qio_ref[:, sl] = rope_b(qi_ref[:, sl]).astype(BF16)
    kw = kw_ref[...]
    kio_ref[...] = rope_b(kw)[:, :IDX_DIM].astype(BF16)
    lane = lax.broadcasted_iota(jnp.int32, kw.shape, 1)
    wi = pltpu.roll(kw, LANES - IDX_DIM, axis=1) * (IDX_HEADS ** -0.5 * IDX_DIM ** -0.5)
    wo_ref[...] = jnp.where(lane < IDX_HEADS, wi, 0.0)


def dsa_prep(z, tables, seq, *, tm=256):
    t = z.shape[0]
    tm = min(tm, seq)
    cur = lambda c, w: pl.BlockSpec((tm, w), lambda i: (i, c // w))
    tab = pl.BlockSpec((tm, LANES), lambda i: (i, 0))
    out = lambda w: pl.BlockSpec((tm, w), lambda i: (i, 0))
    shp = lambda w, dt: jax.ShapeDtypeStruct((t, w), dt)
    return pl.pallas_call(
        _dsa_prep_kernel,
        grid=(t // tm,),
        in_specs=[cur(ZC_Q, B_WIDTH), cur(ZC_QI, IDX_WIDTH), cur(ZC_KB, B_KV_WIDTH),
                  cur(ZC_VB, B_KV_WIDTH), cur(ZC_KW, LANES)] + [tab] * 5,
        out_specs=[out(B_WIDTH), out(IDX_WIDTH), out(B_KV_WIDTH), out(B_KV_WIDTH),
                   out(IDX_DIM), out(LANES)],
        out_shape=[shp(B_WIDTH, BF16), shp(IDX_WIDTH, BF16), shp(B_KV_WIDTH, BF16),
                   shp(B_KV_WIDTH, BF16), shp(IDX_DIM, BF16), shp(LANES, F32)],
        compiler_params=_cparams("parallel"),
        name="dsa_prep",
    )(z, z, z, z, z, *tables)


DSA_TK = 512
NEG_INF_KEY = -2139095041
INT_MAX = 2147483647
INT_MIN = -2147483648


def _dsa_kernel(qi_ref, w_ref, q_ref, ki_ref, k_ref, v_ref, o_ref,
                key_ref, thr_ref, tie_ref, m_ref, l_ref, acc_ref, *, topk, idx_bits):
    tk = DSA_TK
    nq = Q_BLOCK
    q0 = pl.program_id(1) * nq
    n_kt = (q0 + nq + tk - 1) // tk
    row_q = q0 + lax.broadcasted_iota(jnp.int32, (nq, tk), 0)
    lane_k = lax.broadcasted_iota(jnp.int32, (nq, tk), 1)
    neg = -jnp.inf

    w = w_ref[...]

    def score_tile(kt, carry):
        k0 = pl.multiple_of(kt * tk, tk)
        ki_t = ki_ref[pl.ds(k0, tk), :]
        sc = jnp.zeros((nq, tk), F32)
        for h in range(IDX_HEADS):
            lg = lax.dot_general(qi_ref[:, h * IDX_DIM:(h + 1) * IDX_DIM], ki_t,
                                 (((1,), (1,)), ((), ())), preferred_element_type=F32)
            sc = sc + jnp.maximum(lg, 0.0) * w[:, h:h + 1]
        sc = jnp.where(sc == 0.0, 0.0, sc)
        sc = jnp.where(k0 + lane_k <= row_q, sc, neg)
        bits = pltpu.bitcast(sc, jnp.int32)
        key_ref[:, pl.ds(k0, tk)] = jnp.where(bits < 0, bits ^ INT_MAX, bits)
        return carry

    lax.fori_loop(0, n_kt, score_tile, 0)

    def count(pred):
        def body(kt, acc):
            k0 = pl.multiple_of(kt * tk, tk)
            hit = jnp.where(pred(key_ref[:, pl.ds(k0, tk)], k0 + lane_k), 1, 0)
            for j in range(tk // LANES):
                acc = acc + hit[:, j * LANES:(j + 1) * LANES]
            return acc
        acc = lax.fori_loop(0, n_kt, body, jnp.zeros((nq, LANES), jnp.int32))
        return jnp.sum(acc, axis=-1, keepdims=True)

    thr_ref[...] = jnp.full((nq, 1), NEG_INF_KEY + 1, jnp.int32)
    tie_ref[...] = jnp.full((nq, 1), INT_MAX, jnp.int32)

    @pl.when(q0 + nq > topk)
    def _():
        c0 = count(lambda kk, idx: kk >= 0)
        base0 = jnp.where(c0 >= topk, 0, INT_MIN)

        def bit_step(j, base):
            cand = base | (jnp.int32(1) << (30 - j))
            c = count(lambda kk, idx: kk >= cand)
            return jnp.where(c >= topk, cand, base)

        thr = lax.fori_loop(0, 31, bit_step, base0)
        n_gt = count(lambda kk, idx: kk > thr)
        n_eq = count(lambda kk, idx: kk == thr)
        need = topk - n_gt
        narrow = (q0 + 1 + lax.broadcasted_iota(jnp.int32, (nq, 1), 0)) > topk
        thr_ref[...] = jnp.where(narrow, thr, NEG_INF_KEY + 1)
        excess = jnp.where(narrow & (n_eq > need), 1, 0)

        @pl.when(jnp.max(excess) > 0)
        def _():
            def idx_step(j, x):
                cand = x | (jnp.int32(1) << (idx_bits - 1 - j))
                c = count(lambda kk, idx: (kk == thr) & (idx < cand))
                return jnp.where(c < need, cand, x)

            x = lax.fori_loop(0, idx_bits, idx_step, jnp.zeros((nq, 1), jnp.int32))
            tie_ref[...] = jnp.where(excess > 0, x, INT_MAX)

    m_ref[...] = jnp.full(m_ref.shape, neg, F32)
    l_ref[...] = jnp.zeros(l_ref.shape, F32)
    acc_ref[...] = jnp.zeros(acc_ref.shape, F32)
    thr = thr_ref[...]
    tie = tie_ref[...]
    scale = B_HEAD_DIM ** -0.5
    group = B_HEADS // B_KV_HEADS

    def attend_tile(kt, carry):
        k0 = pl.multiple_of(kt * tk, tk)
        keys = key_ref[:, pl.ds(k0, tk)]
        sel = (keys > thr) | ((keys == thr) & (k0 + lane_k <= tie))
        for g in range(B_KV_HEADS):
            k_t = k_ref[pl.ds(k0, tk), g * B_HEAD_DIM:(g + 1) * B_HEAD_DIM]
            v_t = v_ref[pl.ds(k0, tk), g * B_HEAD_DIM:(g + 1) * B_HEAD_DIM]
            for hq in range(group):
                h = g * group + hq
                s = lax.dot_general(q_ref[:, h * B_HEAD_DIM:(h + 1) * B_HEAD_DIM], k_t,
                                    (((1,), (1,)), ((), ())), preferred_element_type=F32) * scale
                s = jnp.where(sel, s, neg)
                m_old = m_ref[h]
                m_new = jnp.maximum(m_old, jnp.max(s, axis=-1, keepdims=True))
                m_safe = jnp.where(m_new == neg, 0.0, m_new)
                alpha = jnp.exp(m_old - m_safe)
                p = jnp.exp(s - m_safe)
                l_ref[h] = alpha * l_ref[h] + jnp.sum(p, axis=-1, keepdims=True)
                acc_ref[h] = alpha * acc_ref[h] + jnp.dot(p.astype(BF16), v_t,
                                                          preferred_element_type=F32)
                m_ref[h] = m_new
        return carry

    lax.fori_loop(0, n_kt, attend_tile, 0)
    for h in range(B_HEADS):
        o_ref[:, h * B_HEAD_DIM:(h + 1) * B_HEAD_DIM] = (acc_ref[h] / l_ref[h]).astype(o_ref.dtype)


def dsa_attend(qi, w, q, ki, k, v, batch, seq):
    nqb = seq // Q_BLOCK
    topk = min(MAX_TOPK, seq // 4)
    idx_bits = max(1, (seq - 1).bit_length())
    qblk = lambda wd: pl.BlockSpec((Q_BLOCK, wd), lambda b, i: (b * nqb + i, 0))
    full = lambda wd: pl.BlockSpec((None, seq, wd), lambda b, i: (b, 0, 0))
    return pl.pallas_call(
        functools.partial(_dsa_kernel, topk=topk, idx_bits=idx_bits),
        grid=(batch, nqb),
        in_specs=[qblk(IDX_WIDTH), qblk(LANES), qblk(B_WIDTH),
                  full(IDX_DIM), full(B_KV_WIDTH), full(B_KV_WIDTH)],
        out_specs=qblk(B_WIDTH),
        out_shape=jax.ShapeDtypeStruct((batch * seq, B_WIDTH), BF16),
        scratch_shapes=[pltpu.VMEM((Q_BLOCK, seq), jnp.int32),
                        pltpu.VMEM((Q_BLOCK, 1), jnp.int32),
                        pltpu.VMEM((Q_BLOCK, 1), jnp.int32),
                        pltpu.VMEM((B_HEADS, Q_BLOCK, 1), F32),
                        pltpu.VMEM((B_HEADS, Q_BLOCK, 1), F32),
                        pltpu.VMEM((B_HEADS, Q_BLOCK, B_HEAD_DIM), F32)],
        compiler_params=_cparams("parallel", "arbitrary"),
        name="dsa_attend",
    )(qi, w, q, ki, k, v)


def _pack_w_in(w):
    o_q = A_COLS
    o_k = o_q + B_WIDTH
    o_v = o_k + B_KV_WIDTH
    o_qi = o_v + B_KV_WIDTH
    o_kw = o_qi + IDX_WIDTH
    parts = [w[:, :3 * A_WIDTH], w[:, o_q:o_k], w[:, o_qi:o_kw], w[:, o_k:o_v], w[:, o_v:o_qi],
             w[:, 3 * A_WIDTH:A_COLS], w[:, o_kw:]]
    used = sum(p.shape[1] for p in parts)
    parts.append(jnp.zeros((w.shape[0], Z_COLS - used), w.dtype))
    return jnp.concatenate(parts, axis=1).astype(BF16)


def _pad_rows(w, r0, rows):
    return jnp.pad(w, ((r0, rows - r0 - w.shape[0]), (0, 0))).astype(BF16)


def kernel(x, c, positions, mod_w, mod_b, mod_table, final_table, norm_g, final_g,
           ev_w_in, ev_mu, ev_w0, ev_w2, ev_a0, ev_a2, ev_g2, ev_k_k, ev_k_a, ev_r_k,
           ev_lnx_g, ev_lnx_b, ev_w_out, ffn_w1, ffn_w3, ffn_w2,
           od_conv_in, od_conv_w, od_conv_out, moe_router, moe_w1, moe_w3, moe_w2):
    bsz, seq, d = x.shape
    t = bsz * seq
    row = lambda v: v.reshape(1, -1)
    per_batch = lambda v: v[:, None, :]

    c8 = jnp.pad(c, ((0, SUBLANES - bsz), (0, 0)))
    t0 = cond_proj(c8, mod_w, row(mod_b))[:bsz].reshape(bsz, N_MOD, d)
    tables = rope_tables(positions)
    xf = x.reshape(t, d)

    for i in range(DEPTH):
        j = i // 2
        mod = t0 + mod_table[i][None]
        mul1 = per_batch(norm_g[i, 0][None] * (1.0 + mod[:, 1]))
        add1 = per_batch(mod[:, 0])
        gate1 = per_batch(mod[:, 2])
        mul2 = per_batch(norm_g[i, 1][None] * (1.0 + mod[:, 4]))
        add2 = per_batch(mod[:, 3])
        gate2 = per_batch(mod[:, 5])

        if i % 2 == 0:
            z = norm_matmul(xf, mul1, add1, _pack_w_in(ev_w_in[j]), seq)
            mu = ev_mu[j]
            aw = A_WIDTH
            r, k, v, ld, a, g = rwkv_prep(
                z, row(mu[:aw]), row(mu[aw:2 * aw]), row(mu[2 * aw:3 * aw]), row(mu[3 * aw:]),
                _pad_rows(ev_w2[j], 0, LORA_COLS), _pad_rows(ev_a2[j], DECAY_LORA, LORA_COLS),
                _pad_rows(ev_g2[j], DECAY_LORA + ICLR_LORA, LORA_COLS),
                row(ev_w0[j]), row(ev_a0[j]), seq)
            b3 = lambda u: u.reshape(bsz, seq, -1)
            ya = rwkv_mix(b3(r), b3(k), b3(v), b3(ld), b3(a), b3(g),
                          row(ev_k_k[j]), row(ev_k_a[j]), row(ev_r_k[j]),
                          row(ev_lnx_g[j]), row(ev_lnx_b[j]))
            q, qi, kb, vb, ki, wi = dsa_prep(z, tables, seq)
            yb = dsa_attend(qi, wi, q, b3(ki), b3(kb), b3(vb), bsz, seq)
            m = jnp.concatenate([ya.reshape(t, aw), yb], axis=1)
            xf = matmul_residual(m, ev_w_out[j].astype(BF16), xf, gate1, seq)
            gf = norm_swiglu(xf, mul2, add2, ffn_w1[j].astype(BF16), ffn_w3[j].astype(BF16), seq)
            xf = matmul_residual(gf, ffn_w2[j].astype(BF16), xf, gate2, seq)
        else:
            bcx = norm_matmul(xf, mul1, add1, od_conv_in[j].astype(BF16), seq)
            cw = jnp.pad(od_conv_w[j], ((0, SUBLANES - CONV_WIDTH), (0, 0)))
            cm = gated_short_conv(bcx, cw, seq)
            xf = matmul_residual(cm, od_conv_out[j].astype(BF16), xf, gate1, seq)
            wr = jnp.pad(moe_router[j], ((0, 0), (0, LANES - N_EXPERTS))).astype(BF16)
            comb = route_top2(xf, mul2, add2, wr, seq)
            y = xf
            for e in range(N_EXPERTS):
                ge = norm_swiglu(xf, mul2, add2, moe_w1[j, e].astype(BF16),
                                 moe_w3[j, e].astype(BF16), seq)
                y = matmul_residual(ge, moe_w2[j, e].astype(BF16), y, gate2, seq,
                                    rs=comb[:, e:e + 1])
            xf = y

    mul_f = per_batch(final_g[None] * (1.0 + t0[:, 1] + final_table[1][None]))
    add_f = per_batch(t0[:, 0] + final_table[0][None])
    return final_norm(xf, mul_f, add_f, seq).reshape(bsz, seq, d)
```

```python
import functools

import jax
import jax.numpy as jnp
from jax import lax
from jax.experimental import pallas as pl
from jax.experimental.pallas import tpu as pltpu

F32 = jnp.float32
BF16 = jnp.bfloat16

D_MODEL = 2048
DEPTH = 4
A_HEADS = 16
A_HEAD_DIM = 64
A_WIDTH = A_HEADS * A_HEAD_DIM
DECAY_LORA = 64
ICLR_LORA = 64
GATE_LORA = 128
LORA_COLS = DECAY_LORA + ICLR_LORA + GATE_LORA
A_COLS = 3 * A_WIDTH + LORA_COLS
B_HEADS = 8
B_KV_HEADS = 2
B_HEAD_DIM = 128
B_WIDTH = B_HEADS * B_HEAD_DIM
B_KV_WIDTH = B_KV_HEADS * B_HEAD_DIM
IDX_HEADS = 16
IDX_DIM = 64
IDX_WIDTH = IDX_HEADS * IDX_DIM
MAX_TOPK = 256
Q_BLOCK = 128
CONV_WIDTH = 3
D_FF = 5632
N_EXPERTS = 8
ROPE_THETA = 10000.0
NORM_EPS = 1e-6
GN_EPS = 64e-5
N_MOD = 6

LANES = 128
SUBLANES = 8
VMEM_LIMIT = 56 << 20

ZC_R, ZC_K, ZC_V, ZC_Q, ZC_QI = 0, 1024, 2048, 3072, 4096
ZC_KB, ZC_VB, ZC_LORA, ZC_KW = 5120, 5376, 5632, 5888
Z_COLS = 6144


def _cparams(*sem):
    return pltpu.CompilerParams(dimension_semantics=sem, vmem_limit_bytes=VMEM_LIMIT)


NORM_ROWS = 256


def _norm_rows_into(x_ref, mul_ref, add_ref, h_ref):
    tm = x_ref.shape[0]

    def body(c, carry):
        r0 = pl.multiple_of(c * NORM_ROWS, NORM_ROWS)
        x = x_ref[pl.ds(r0, NORM_ROWS), :]
        ms = jnp.mean(x * x, axis=-1, keepdims=True)
        y = x * lax.rsqrt(ms + NORM_EPS)
        h_ref[pl.ds(r0, NORM_ROWS), :] = (y * mul_ref[...] + add_ref[...]).astype(h_ref.dtype)
        return carry

    lax.fori_loop(0, tm // NORM_ROWS, body, 0)


def _norm_mm_kernel(x_ref, mul_ref, add_ref, w_ref, o_ref, h_ref):
    @pl.when(pl.program_id(1) == 0)
    def _():
        _norm_rows_into(x_ref, mul_ref, add_ref, h_ref)

    o_ref[...] = jnp.dot(h_ref[...], w_ref[...], preferred_element_type=F32).astype(o_ref.dtype)


def norm_matmul(x, mul, add, w, seq, *, tm=1024, tn=512, out_dtype=F32):
    t, d = x.shape
    n = w.shape[1]
    tm = min(tm, seq)
    bpb = seq // tm
    return pl.pallas_call(
        _norm_mm_kernel,
        grid=(t // tm, n // tn),
        in_specs=[
            pl.BlockSpec((tm, d), lambda i, j: (i, 0)),
            pl.BlockSpec((None, 1, d), lambda i, j: (i // bpb, 0, 0)),
            pl.BlockSpec((None, 1, d), lambda i, j: (i // bpb, 0, 0)),
            pl.BlockSpec((d, tn), lambda i, j: (0, j)),
        ],
        out_specs=pl.BlockSpec((tm, tn), lambda i, j: (i, j)),
        out_shape=jax.ShapeDtypeStruct((t, n), out_dtype),
        scratch_shapes=[pltpu.VMEM((tm, d), BF16)],
        compiler_params=_cparams("parallel", "arbitrary"),
        name="norm_matmul",
    )(x, mul, add, w)


def _norm_swiglu_kernel(x_ref, mul_ref, add_ref, w1_ref, w3_ref, o_ref, h_ref):
    @pl.when(pl.program_id(1) == 0)
    def _():
        _norm_rows_into(x_ref, mul_ref, add_ref, h_ref)

    h = h_ref[...]
    u = jnp.dot(h, w1_ref[...], preferred_element_type=F32)
    g = jnp.dot(h, w3_ref[...], preferred_element_type=F32)
    o_ref[...] = (u * jax.nn.sigmoid(u) * g).astype(o_ref.dtype)


def norm_swiglu(x, mul, add, w1, w3, seq, *, tm=1024, tn=512):
    t, d = x.shape
    n = w1.shape[1]
    tm = min(tm, seq)
    bpb = seq // tm
    return pl.pallas_call(
        _norm_swiglu_kernel,
        grid=(t // tm, n // tn),
        in_specs=[
            pl.BlockSpec((tm, d), lambda i, j: (i, 0)),
            pl.BlockSpec((None, 1, d), lambda i, j: (i // bpb, 0, 0)),
            pl.BlockSpec((None, 1, d), lambda i, j: (i // bpb, 0, 0)),
            pl.BlockSpec((d, tn), lambda i, j: (0, j)),
            pl.BlockSpec((d, tn), lambda i, j: (0, j)),
        ],
        out_specs=pl.BlockSpec((tm, tn), lambda i, j: (i, j)),
        out_shape=jax.ShapeDtypeStruct((t, n), BF16),
        scratch_shapes=[pltpu.VMEM((tm, d), BF16)],
        compiler_params=_cparams("parallel", "arbitrary"),
        name="norm_swiglu",
    )(x, mul, add, w1, w3)


def _mm_res_kernel(a_ref, w_ref, base_ref, cs_ref, o_ref):
    acc = jnp.dot(a_ref[...], w_ref[...], preferred_element_type=F32)
    o_ref[...] = base_ref[...] + cs_ref[...] * acc


def matmul_residual(a, w, base, cs, seq, *, tm=1024, tn=256):
    t, k = a.shape
    n = w.shape[1]
    tm = min(tm, seq)
    bpb = seq // tm
    return pl.pallas_call(
        _mm_res_kernel,
        grid=(t // tm, n // tn),
        in_specs=[
            pl.BlockSpec((tm, k), lambda i, j: (i, 0)),
            pl.BlockSpec((k, tn), lambda i, j: (0, j)),
            pl.BlockSpec((tm, tn), lambda i, j: (i, j)),
            pl.BlockSpec((None, 1, tn), lambda i, j: (i // bpb, 0, j)),
        ],
        out_specs=pl.BlockSpec((tm, tn), lambda i, j: (i, j)),
        out_shape=jax.ShapeDtypeStruct((t, n), F32),
        compiler_params=_cparams("parallel", "arbitrary"),
        name="matmul_residual",
    )(a, w, base, cs)


def _final_norm_kernel(x_ref, mul_ref, add_ref, o_ref):
    x = x_ref[...]
    ms = jnp.mean(x * x, axis=-1, keepdims=True)
    o_ref[...] = x * lax.rsqrt(ms + NORM_EPS) * mul_ref[...] + add_ref[...]


def final_norm(x, mul, add, seq, *, tm=256):
    t, d = x.shape
    bpb = seq // tm
    return pl.pallas_call(
        _final_norm_kernel,
        grid=(t // tm,),
        in_specs=[
            pl.BlockSpec((tm, d), lambda i: (i, 0)),
            pl.BlockSpec((None, 1, d), lambda i: (i // bpb, 0, 0)),
            pl.BlockSpec((None, 1, d), lambda i: (i // bpb, 0, 0)),
        ],
        out_specs=pl.BlockSpec((tm, d), lambda i: (i, 0)),
        out_shape=jax.ShapeDtypeStruct((t, d), F32),
        compiler_params=_cparams("parallel"),
        name="final_norm",
    )(x, mul, add)


def _shift_rows(y, prev8, k):
    rolled = pltpu.roll(y, k, axis=0)
    row = lax.broadcasted_iota(jnp.int32, y.shape, 0)
    for r in range(k):
        rolled = jnp.where(row == r, prev8[SUBLANES - k + r:SUBLANES - k + r + 1, :], rolled)
    return rolled


def _conv_kernel(bg_ref, cg_ref, u_ref, cgp_ref, up_ref, w_ref, o_ref, *, bpb):
    first = (pl.program_id(0) % bpb) == 0
    y = cg_ref[...] * u_ref[...]
    yp = jnp.where(first, 0.0, cgp_ref[...] * up_ref[...])
    w = w_ref[...]
    conv = w[2:3, :] * y + w[1:2, :] * _shift_rows(y, yp, 1) + w[0:1, :] * _shift_rows(y, yp, 2)
    o_ref[...] = (bg_ref[...] * conv).astype(o_ref.dtype)


def gated_short_conv(bcx, conv_w, seq, *, tm=512):
    t = bcx.shape[0]
    d = D_MODEL
    tm = min(tm, seq)
    bpb = seq // tm
    r8 = tm // SUBLANES
    prev = lambda c: (lambda i: (jnp.maximum(i * r8 - 1, 0), c))
    return pl.pallas_call(
        functools.partial(_conv_kernel, bpb=bpb),
        grid=(t // tm,),
        in_specs=[
            pl.BlockSpec((tm, d), lambda i: (i, 0)),
            pl.BlockSpec((tm, d), lambda i: (i, 1)),
            pl.BlockSpec((tm, d), lambda i: (i, 2)),
            pl.BlockSpec((SUBLANES, d), prev(1)),
            pl.BlockSpec((SUBLANES, d), prev(2)),
            pl.BlockSpec((SUBLANES, d), lambda i: (0, 0)),
        ],
        out_specs=pl.BlockSpec((tm, d), lambda i: (i, 0)),
        out_shape=jax.ShapeDtypeStruct((t, d), BF16),
        compiler_params=_cparams("parallel"),
        name="gated_short_conv",
    )(bcx, bcx, bcx, bcx, bcx, conv_w)


def _cond_kernel(c_ref, w_ref, b_ref, o_ref):
    c = c_ref[...]
    a = (c * jax.nn.sigmoid(c)).astype(BF16)
    o_ref[...] = jnp.dot(a, w_ref[...].astype(BF16), preferred_element_type=F32) + b_ref[...]


def cond_proj(c8, w, b, *, tn=1024):
    d, n = w.shape
    return pl.pallas_call(
        _cond_kernel,
        grid=(n // tn,),
        in_specs=[pl.BlockSpec((SUBLANES, d), lambda j: (0, 0)),
                  pl.BlockSpec((d, tn), lambda j: (0, j)),
                  pl.BlockSpec((1, tn), lambda j: (0, j))],
        out_specs=pl.BlockSpec((SUBLANES, tn), lambda j: (0, j)),
        out_shape=jax.ShapeDtypeStruct((SUBLANES, n), F32),
        compiler_params=_cparams("parallel"),
        name="cond_proj",
    )(c8, w, b)


ROUTE_IDX_LANE = N_EXPERTS
ROUTE_GATE_LANE = N_EXPERTS + 2


def _router_kernel(x_ref, mul_ref, add_ref, wr_ref, o_ref, h_ref):
    x = x_ref[...]
    ms = jnp.mean(x * x, axis=-1, keepdims=True)
    h = x * lax.rsqrt(ms + NORM_EPS) * mul_ref[...] + add_ref[...]
    h_ref[...] = h
    lg = jnp.dot(h.astype(BF16), wr_ref[...], preferred_element_type=F32)
    lane = lax.broadcasted_iota(jnp.int32, lg.shape, 1)
    neg = -jnp.inf
    lg = jnp.where(lane < N_EXPERTS, lg, neg)
    m1 = jnp.max(lg, axis=-1, keepdims=True)
    i1 = jnp.min(jnp.where(lg == m1, lane, LANES), axis=-1, keepdims=True)
    lg2 = jnp.where(lane == i1, neg, lg)
    m2 = jnp.max(lg2, axis=-1, keepdims=True)
    i2 = jnp.min(jnp.where(lg2 == m2, lane, LANES), axis=-1, keepdims=True)
    e = jnp.exp(m2 - m1)
    den = 1.0 + e
    g1 = 1.0 / den
    g2 = e / den
    out = jnp.where(lane == i1, g1, 0.0) + jnp.where(lane == i2, g2, 0.0)
    out = jnp.where(lane == ROUTE_IDX_LANE, i1.astype(F32), out)
    out = jnp.where(lane == ROUTE_IDX_LANE + 1, i2.astype(F32), out)
    out = jnp.where(lane == ROUTE_GATE_LANE, g1, out)
    o_ref[...] = jnp.where(lane == ROUTE_GATE_LANE + 1, g2, out)


def route_top2(x, mul, add, wr, seq, *, tm=256):
    t, d = x.shape
    tm = min(tm, seq)
    bpb = seq // tm
    return pl.pallas_call(
        _router_kernel,
        grid=(t // tm,),
        in_specs=[
            pl.BlockSpec((tm, d), lambda i: (i, 0)),
            pl.BlockSpec((None, 1, d), lambda i: (i // bpb, 0, 0)),
            pl.BlockSpec((None, 1, d), lambda i: (i // bpb, 0, 0)),
            pl.BlockSpec((d, LANES), lambda i: (0, 0)),
        ],
        out_specs=[pl.BlockSpec((tm, LANES), lambda i: (i, 0)),
                   pl.BlockSpec((tm, d), lambda i: (i, 0))],
        out_shape=[jax.ShapeDtypeStruct((t, LANES), F32), jax.ShapeDtypeStruct((t, d), F32)],
        compiler_params=_cparams("parallel"),
        name="moe_router",
    )(x, mul, add, wr)


MOE_TM = 512
COMBINE_TM = 256


def _row_copy(src_ref, src_row, dst_ref, dst_row, sem):
    return pltpu.make_async_copy(src_ref.at[pl.ds(src_row, 1), :],
                                 dst_ref.at[pl.ds(dst_row, 1), :], sem)


def _gather_into(idx_ref, src_ref, dst_ref, dst0, n, sem):
    def start(r, carry):
        _row_copy(src_ref, idx_ref[0, r], dst_ref, dst0 + r, sem).start()
        return carry

    def wait(r, carry):
        _row_copy(src_ref, 0, dst_ref, dst0 + r, sem).wait()
        return carry

    lax.fori_loop(0, n, start, 0)
    lax.fori_loop(0, n, wait, 0)


def _gather_rows_kernel(idx_ref, src_ref, o_ref, sem):
    _gather_into(idx_ref, src_ref, o_ref, 0, o_ref.shape[0], sem)


def gather_rows(src, idx3):
    n_tiles, _, tm = idx3.shape
    d = src.shape[1]
    return pl.pallas_call(
        _gather_rows_kernel,
        grid=(n_tiles,),
        in_specs=[pl.BlockSpec((None, 1, tm), lambda i: (i, 0, 0), memory_space=pltpu.SMEM),
                  pl.BlockSpec(memory_space=pl.ANY)],
        out_specs=pl.BlockSpec((tm, d), lambda i: (i, 0)),
        out_shape=jax.ShapeDtypeStruct((n_tiles * tm, d), src.dtype),
        scratch_shapes=[pltpu.SemaphoreType.DMA(())],
        compiler_params=_cparams("arbitrary"),
        name="moe_gather_rows",
    )(idx3, src)


def _expert_up_kernel(te_ref, x_ref, w1_ref, w3_ref, o_ref, h_ref):
    @pl.when(pl.program_id(1) == 0)
    def _():
        h_ref[...] = x_ref[...].astype(BF16)

    h = h_ref[...]
    u = jnp.dot(h, w1_ref[...], preferred_element_type=F32)
    g = jnp.dot(h, w3_ref[...], preferred_element_type=F32)
    o_ref[...] = (u * jax.nn.sigmoid(u) * g).astype(o_ref.dtype)


def expert_up(tile_expert, hs, w1, w3, *, tn=512):
    p, d = hs.shape
    ff = w1.shape[2]
    tm = MOE_TM
    wspec = pl.BlockSpec((None, d, tn), lambda i, j, te: (te[i], 0, j))
    return pl.pallas_call(
        _expert_up_kernel,
        grid_spec=pltpu.PrefetchScalarGridSpec(
            num_scalar_prefetch=1,
            grid=(p // tm, ff // tn),
            in_specs=[pl.BlockSpec((tm, d), lambda i, j, te: (i, 0)), wspec, wspec],
            out_specs=pl.BlockSpec((tm, tn), lambda i, j, te: (i, j)),
            scratch_shapes=[pltpu.VMEM((tm, d), BF16)]),
        out_shape=jax.ShapeDtypeStruct((p, ff), BF16),
        compiler_params=_cparams("parallel", "arbitrary"),
        name="moe_expert_up",
    )(tile_expert, hs, w1, w3)


def _expert_down_kernel(te_ref, g_ref, w_ref, rs_ref, o_ref):
    o_ref[...] = rs_ref[...] * jnp.dot(g_ref[...], w_ref[...], preferred_element_type=F32)


def expert_down(tile_expert, g, w2, row_gate, *, tn=512):
    p, ff = g.shape
    d = w2.shape[2]
    tm = MOE_TM
    return pl.pallas_call(
        _expert_down_kernel,
        grid_spec=pltpu.PrefetchScalarGridSpec(
            num_scalar_prefetch=1,
            grid=(p // tm, d // tn),
            in_specs=[pl.BlockSpec((tm, ff), lambda i, j, te: (i, 0)),
                      pl.BlockSpec((None, ff, tn), lambda i, j, te: (te[i], 0, j)),
                      pl.BlockSpec((tm, 1), lambda i, j, te: (i, 0))],
            out_specs=pl.BlockSpec((tm, tn), lambda i, j, te: (i, j))),
        out_shape=jax.ShapeDtypeStruct((p, d), F32),
        compiler_params=_cparams("parallel", "arbitrary"),
        name="moe_expert_down",
    )(tile_expert, g, w2, row_gate)


def _combine_kernel(p1_ref, p2_ref, ys_ref, x_ref, cs_ref, o_ref, buf_ref, sem):
    tm = x_ref.shape[0]
    _gather_into(p1_ref, ys_ref, buf_ref, 0, tm, sem)
    _gather_into(p2_ref, ys_ref, buf_ref, tm, tm, sem)
    o_ref[...] = x_ref[...] + cs_ref[...] * (buf_ref[pl.ds(0, tm), :] + buf_ref[pl.ds(tm, tm), :])


def combine_pairs(ys, pos1, pos2, x, cs, seq):
    t, d = x.shape
    tm = pos1.shape[2]
    bpb = seq // tm
    idx = pl.BlockSpec((None, 1, tm), lambda i: (i, 0, 0), memory_space=pltpu.SMEM)
    return pl.pallas_call(
        _combine_kernel,
        grid=(t // tm,),
        in_specs=[idx, idx, pl.BlockSpec(memory_space=pl.ANY),
                  pl.BlockSpec((tm, d), lambda i: (i, 0)),
                  pl.BlockSpec((None, 1, d), lambda i: (i // bpb, 0, 0))],
        out_specs=pl.BlockSpec((tm, d), lambda i: (i, 0)),
        out_shape=jax.ShapeDtypeStruct((t, d), F32),
        scratch_shapes=[pltpu.VMEM((2 * tm, d), F32), pltpu.SemaphoreType.DMA(())],
        compiler_params=_cparams("arbitrary"),
        name="moe_combine",
    )(pos1, pos2, ys, x, cs)


def _route_plan(rec, t):
    tm = MOE_TM
    n_tiles = (2 * t) // tm + N_EXPERTS
    p = n_tiles * tm
    e1 = rec[:, ROUTE_IDX_LANE].astype(jnp.int32)
    e2 = rec[:, ROUTE_IDX_LANE + 1].astype(jnp.int32)
    expert = jnp.concatenate([e1, e2])
    gate = jnp.concatenate([rec[:, ROUTE_GATE_LANE], rec[:, ROUTE_GATE_LANE + 1]])
    token = jnp.tile(jnp.arange(t, dtype=jnp.int32), 2)
    order = jnp.argsort(expert, stable=True)
    counts = jnp.zeros((N_EXPERTS,), jnp.int32).at[expert].add(1)
    padded = ((counts + tm - 1) // tm) * tm
    start = jnp.cumsum(counts) - counts
    pstart = jnp.cumsum(padded) - padded
    se = expert[order]
    dest = pstart[se] + (jnp.arange(2 * t, dtype=jnp.int32) - start[se])
    row_token = jnp.zeros((p,), jnp.int32).at[dest].set(token[order])
    row_gate = jnp.zeros((p,), F32).at[dest].set(gate[order])
    pos = jnp.zeros((2 * t,), jnp.int32).at[order].set(dest)
    tile_expert = jnp.minimum(
        jnp.searchsorted(jnp.cumsum(padded), jnp.arange(n_tiles, dtype=jnp.int32) * tm, side="right"),
        N_EXPERTS - 1).astype(jnp.int32)
    return row_token, row_gate, pos[:t], pos[t:], tile_expert


def moe_block(x, mul, add, wr, w1, w3, w2, cs, seq):
    t = x.shape[0]
    rec, h = route_top2(x, mul, add, wr, seq)
    row_token, row_gate, pos1, pos2, tile_expert = _route_plan(rec, t)
    hs = gather_rows(h, row_token.reshape(-1, 1, MOE_TM))
    g = expert_up(tile_expert, hs, w1, w3)
    ys = expert_down(tile_expert, g, w2, row_gate[:, None])
    ctm = min(COMBINE_TM, seq)
    return combine_pairs(ys, pos1.reshape(-1, 1, ctm), pos2.reshape(-1, 1, ctm), x, cs, seq)


def _mm(x, y):
    return jnp.dot(x.astype(BF16), y.astype(BF16), preferred_element_type=F32)


def _mm_nt(x, y):
    return lax.dot_general(x.astype(BF16), y.astype(BF16), (((1,), (1,)), ((), ())),
                           preferred_element_type=F32)


def _mm_tn(x, y):
    return lax.dot_general(x.astype(BF16), y.astype(BF16), (((0,), (0,)), ((), ())),
                           preferred_element_type=F32)


def _mm_split(x, w):
    hi = x.astype(BF16)
    lo = (x - hi.astype(F32)).astype(BF16)
    return (jnp.dot(hi, w, preferred_element_type=F32)
            + jnp.dot(lo, w, preferred_element_type=F32))


def _rwkv_prep_kernel(r_ref, k_ref, v_ref, lo_ref, rp_ref, kp_ref, vp_ref, lop_ref,
                      mur_ref, muk_ref, muv_ref, mul_ref, w2_ref, a2_ref, g2_ref, w0_ref, a0_ref,
                      ro_ref, ko_ref, vo_ref, ldo_ref, ao_ref, go_ref, *, bpb):
    first = (pl.program_id(0) % bpb) == 0

    def shifted(z_ref, p_ref, mu_ref):
        z = z_ref[...]
        zp = _shift_rows(z, jnp.where(first, 0.0, p_ref[...]), 1)
        return z + (zp - z) * mu_ref[...]

    ro_ref[...] = shifted(r_ref, rp_ref, mur_ref)
    ko_ref[...] = shifted(k_ref, kp_ref, muk_ref)
    vo_ref[...] = shifted(v_ref, vp_ref, muv_ref)
    lora = shifted(lo_ref, lop_ref, mul_ref)
    wl = w0_ref[...] + _mm(jnp.tanh(lora), w2_ref[...])
    softplus = jnp.maximum(-wl, 0.0) + jnp.log(1.0 + jnp.exp(-jnp.abs(wl)))
    ldo_ref[...] = -jnp.exp(-softplus - 0.5)
    sg = jax.nn.sigmoid(lora)
    ao_ref[...] = jax.nn.sigmoid(a0_ref[...] + _mm(lora, a2_ref[...]))
    go_ref[...] = _mm(sg, g2_ref[...])


def rwkv_prep(z, mu_r, mu_k, mu_v, mu_l, w2p, a2p, g2p, w0, a0, seq, *, tm=256):
    t = z.shape[0]
    tm = min(tm, seq)
    bpb = seq // tm
    r8 = tm // SUBLANES
    aw, lw = A_WIDTH, LORA_COLS
    cur = lambda c, w: pl.BlockSpec((tm, w), lambda i: (i, c // w))
    prev = lambda c, w: pl.BlockSpec((SUBLANES, w), lambda i: (jnp.maximum(i * r8 - 1, 0), c // w))
    vec = lambda w: pl.BlockSpec((1, w), lambda i: (0, 0))
    full = lambda a, b: pl.BlockSpec((a, b), lambda i: (0, 0))
    out = pl.BlockSpec((tm, aw), lambda i: (i, 0))
    return pl.pallas_call(
        functools.partial(_rwkv_prep_kernel, bpb=bpb),
        grid=(t // tm,),
        in_specs=[cur(ZC_R, aw), cur(ZC_K, aw), cur(ZC_V, aw), cur(ZC_LORA, lw),
                  prev(ZC_R, aw), prev(ZC_K, aw), prev(ZC_V, aw), prev(ZC_LORA, lw),
                  vec(aw), vec(aw), vec(aw), vec(lw),
                  full(lw, aw), full(lw, aw), full(lw, aw), vec(aw), vec(aw)],
        out_specs=[out] * 6,
        out_shape=[jax.ShapeDtypeStruct((t, aw), F32)] * 6,
        compiler_params=_cparams("parallel"),
        name="rwkv_prep",
    )(z, z, z, z, z, z, z, z, mu_r, mu_k, mu_v, mu_l, w2p, a2p, g2p, w0, a0)


RWKV_CHUNK = 64
RWKV_GROUP_HEADS = 4
RWKV_PAIR = RWKV_GROUP_HEADS * A_HEAD_DIM


def _rwkv_chunk_kernel(r_ref, k_ref, v_ref, ld_ref, a_ref, g_ref,
                       kkw_ref, kaw_ref, rkw_ref, lng_ref, lnb_ref, o_ref, s_ref, *, n_chunks,
                       n_pairs):
    c_len, w = RWKV_CHUNK, RWKV_PAIR

    @pl.when(pl.program_id(2) == 0)
    def _():
        s_ref[...] = jnp.zeros_like(s_ref)

    row = lax.broadcasted_iota(jnp.int32, (w, w), 0)
    col = lax.broadcasted_iota(jnp.int32, (w, w), 1)
    same = (row >> 6) == (col >> 6)
    rt_, ct_ = row & (c_len - 1), col & (c_len - 1)
    strict = same & (ct_ < rt_)
    incl = same & (ct_ <= rt_)
    eye = jnp.where(row == col, 1.0, 0.0)
    ones_bd = jnp.where(same, 1.0, 0.0).astype(BF16)
    tr = lax.broadcasted_iota(jnp.int32, (c_len, c_len), 0)
    tc = lax.broadcasted_iota(jnp.int32, (c_len, c_len), 1)
    tri = jnp.where(tc <= tr, 1.0, 0.0).astype(BF16)
    lane_head = lax.broadcasted_iota(jnp.int32, (c_len, w), 1) >> 6

    def stack(x):
        return jnp.concatenate([jnp.where(lane_head == hd, x, 0.0)
                                for hd in range(RWKV_GROUP_HEADS)], axis=0)

    def unstack(x):
        out = x[:c_len]
        for hd in range(1, RWKV_GROUP_HEADS):
            out = out + x[hd * c_len:(hd + 1) * c_len]
        return out

    inv_n = 1.0 / A_HEAD_DIM

    def solve_pair(sl, pr):
        ln = slice(pr * w, (pr + 1) * w)
        rc, kc, vc, ld, ac = (r_ref[sl, ln], k_ref[sl, ln], v_ref[sl, ln], ld_ref[sl, ln],
                              a_ref[sl, ln])
        kkw, kaw, rkw = kkw_ref[:, ln], kaw_ref[:, ln], rkw_ref[:, ln]
        lng, lnb = lng_ref[:, ln], lnb_ref[:, ln]
        kk = kc * kkw
        kkn = kk / jnp.maximum(jnp.sqrt(_mm_split(kk * kk, ones_bd)), 1e-12)
        kf = kc * (1.0 + (ac - 1.0) * kaw)
        ld_hi = ld.astype(BF16)
        ld_lo = (ld - ld_hi.astype(F32)).astype(BF16)
        logp = (jnp.dot(tri, ld_hi, preferred_element_type=F32)
                + jnp.dot(tri, ld_lo, preferred_element_type=F32))
        p = jnp.exp(logp)
        inv_p = jnp.exp(-logp)
        p_prev = jnp.exp(logp - ld)
        at_s = stack(-kkn * p_prev).astype(BF16)
        bt_s = stack(kkn * ac * inv_p).astype(BF16)
        kt_s = stack(kf * inv_p).astype(BF16)
        rt = stack(rc * p)
        rt_s = rt.astype(BF16)
        v_s = stack(vc).astype(BF16)

        a_ab = jnp.where(strict, _mm_nt(at_s, bt_s), 0.0)
        a_ak = jnp.where(strict, _mm_nt(at_s, kt_s), 0.0)
        y_ab = jnp.where(incl, _mm_nt(rt_s, bt_s), 0.0)
        y_ak = jnp.where(incl, _mm_nt(rt_s, kt_s), 0.0)
        tinv = eye + a_ab
        pw = a_ab
        n = 1
        while 2 * n < c_len:
            pw = _mm(pw, pw)
            tinv = tinv + _mm(tinv, pw)
            n *= 2
        w_ = _mm(tinv, at_s)
        u0 = _mm(tinv, _mm(a_ak, v_s))
        rq = rt + _mm(y_ab, w_)
        y0 = _mm(y_ab, u0) + _mm(y_ak, v_s)
        gmat = _mm_tn(w_, bt_s)
        npre = _mm_tn(u0, bt_s) + _mm_tn(v_s, kt_s)
        p_end = p[c_len - 1:c_len, :]

        s = s_ref[pr]
        y = unstack(_mm_nt(rq, s) + y0)
        s_new = (s + _mm(s, gmat) + npre) * p_end

        mu = _mm_split(y, ones_bd) * inv_n
        yc = y - mu
        var = _mm_split(yc * yc, ones_bd) * inv_n
        yn = yc * lax.rsqrt(var + GN_EPS) * lng + lnb
        bonus = _mm_split(rc * kf * rkw, ones_bd) * vc
        return s_new, ((yn + bonus) * g_ref[sl, ln]).astype(o_ref.dtype)

    def chunk(c, carry):
        sl = pl.ds(pl.multiple_of(c * c_len, c_len), c_len)
        solved = [solve_pair(sl, pr) for pr in range(n_pairs)]
        for pr, (s_new, out) in enumerate(solved):
            s_ref[pr] = s_new
            o_ref[sl, pr * w:(pr + 1) * w] = out
        return carry

    lax.fori_loop(0, n_chunks, chunk, 0)


RWKV_PAIRS_PER_STEP = 1


def rwkv_mix(r, k, v, ld, a, g, kkw, kaw, rkw, lng, lnb, *, tl=256):
    b, s, aw = r.shape
    tl = min(tl, s)
    n_pairs = RWKV_PAIRS_PER_STEP
    w = RWKV_PAIR * n_pairs
    tok = pl.BlockSpec((None, tl, w), lambda bi, hi, si: (bi, si, hi))
    par = pl.BlockSpec((1, w), lambda bi, hi, si: (0, hi))
    return pl.pallas_call(
        functools.partial(_rwkv_chunk_kernel, n_chunks=tl // RWKV_CHUNK, n_pairs=n_pairs),
        grid=(b, aw // w, s // tl),
        in_specs=[tok] * 6 + [par] * 5,
        out_specs=tok,
        out_shape=jax.ShapeDtypeStruct((b, s, aw), BF16),
        scratch_shapes=[pltpu.VMEM((n_pairs, RWKV_PAIR, RWKV_PAIR), F32)],
        compiler_params=_cparams("parallel", "parallel", "arbitrary"),
        name="rwkv_mix",
    )(r, k, v, ld, a, g, kkw, kaw, rkw, lng, lnb)


def rope_tables(positions):
    pos = positions.reshape(-1).astype(F32)[:, None]

    def cs(half):
        inv = jnp.power(ROPE_THETA, -jnp.arange(half, dtype=F32) / half)
        ang = pos * inv
        return jnp.cos(ang), jnp.sin(ang)

    c64, s64 = cs(B_HEAD_DIM // 2)
    c32, s32 = cs(IDX_DIM // 2)
    z32 = jnp.zeros_like(s32)
    cat = lambda *xs: jnp.concatenate(xs, axis=-1)
    return (cat(c64, c64), cat(-s64, s64), cat(c32, c32, c32, c32),
            cat(-s32, z32, -s32, z32), cat(z32, s32, z32, s32))


def _dsa_prep_kernel(q_ref, qi_ref, kb_ref, vb_ref, kw_ref, ca_ref, sa_ref, cb_ref, sb1_ref, sb2_ref,
                     qo_ref, qio_ref, ko_ref, vo_ref, kio_ref, wo_ref):
    ca, sa = ca_ref[...], sa_ref[...]
    cb, sb1, sb2 = cb_ref[...], sb1_ref[...], sb2_ref[...]

    def rope_a(x):
        return x * ca + pltpu.roll(x, LANES // 2, axis=1) * sa

    def rope_b(x):
        return (x * cb + pltpu.roll(x, LANES - IDX_DIM // 2, axis=1) * sb1
                + pltpu.roll(x, IDX_DIM // 2, axis=1) * sb2)

    for h in range(B_WIDTH // LANES):
        sl = slice(h * LANES, (h + 1) * LANES)
        qo_ref[:, sl] = rope_a(q_ref[:, sl]).astype(BF16)
    for h in range(B_KV_WIDTH // LANES):
        sl = slice(h * LANES, (h + 1) * LANES)
        ko_ref[:, sl] = rope_a(kb_ref[:, sl]).astype(BF16)
    vo_ref[...] = vb_ref[...].astype(BF16)
    for h in range(IDX_WIDTH // LANES):
        sl = slice(h * LANES, (h + 1) * LANES)
        qio_ref[:, sl] = rope_b(qi_ref[:, sl]).astype(BF16)
    kw = kw_ref[...]
    kio_ref[...] = rope_b(kw)[:, :IDX_DIM].astype(BF16)
    lane = lax.broadcasted_iota(jnp.int32, kw.shape, 1)
    wi = pltpu.roll(kw, LANES - IDX_DIM, axis=1) * (IDX_HEADS ** -0.5 * IDX_DIM ** -0.5)
    wo_ref[...] = jnp.where(lane < IDX_HEADS, wi, 0.0)


def dsa_prep(z, tables, seq, *, tm=256):
    t = z.shape[0]
    tm = min(tm, seq)
    cur = lambda c, w: pl.BlockSpec((tm, w), lambda i: (i, c // w))
    tab = pl.BlockSpec((tm, LANES), lambda i: (i, 0))
    out = lambda w: pl.BlockSpec((tm, w), lambda i: (i, 0))
    shp = lambda w, dt: jax.ShapeDtypeStruct((t, w), dt)
    return pl.pallas_call(
        _dsa_prep_kernel,
        grid=(t // tm,),
        in_specs=[cur(ZC_Q, B_WIDTH), cur(ZC_QI, IDX_WIDTH), cur(ZC_KB, B_KV_WIDTH),
                  cur(ZC_VB, B_KV_WIDTH), cur(ZC_KW, LANES)] + [tab] * 5,
        out_specs=[out(B_WIDTH), out(IDX_WIDTH), out(B_KV_WIDTH), out(B_KV_WIDTH),
                   out(IDX_DIM), out(LANES)],
        out_shape=[shp(B_WIDTH, BF16), shp(IDX_WIDTH, BF16), shp(B_KV_WIDTH, BF16),
                   shp(B_KV_WIDTH, BF16), shp(IDX_DIM, BF16), shp(LANES, F32)],
        compiler_params=_cparams("parallel"),
        name="dsa_prep",
    )(z, z, z, z, z, *tables)


DSA_TK = 1024
NEG_INF_KEY = -2139095041
INT_MAX = 2147483647
INT_MIN = -2147483648


def _dsa_kernel(qi_ref, w_ref, q_ref, ki_ref, k_ref, v_ref, o_ref,
                key_ref, thr_ref, tie_ref, m_ref, l_ref, acc_ref, *, topk, idx_bits):
    tk = DSA_TK
    nq = Q_BLOCK
    q0 = pl.program_id(1) * nq
    n_kt = (q0 + nq + tk - 1) // tk
    row_q = q0 + lax.broadcasted_iota(jnp.int32, (nq, tk), 0)
    lane_k = lax.broadcasted_iota(jnp.int32, (nq, tk), 1)
    neg = -jnp.inf

    w = w_ref[...]

    def score_tile(kt, carry):
        k0 = pl.multiple_of(kt * tk, tk)
        ki_t = ki_ref[pl.ds(k0, tk), :]
        sc = jnp.zeros((nq, tk), F32)
        for h in range(IDX_HEADS):
            lg = lax.dot_general(qi_ref[:, h * IDX_DIM:(h + 1) * IDX_DIM], ki_t,
                                 (((1,), (1,)), ((), ())), preferred_element_type=F32)
            sc = sc + jnp.maximum(lg, 0.0) * w[:, h:h + 1]
        sc = jnp.where(sc == 0.0, 0.0, sc)
        sc = jnp.where(k0 + lane_k <= row_q, sc, neg)
        bits = pltpu.bitcast(sc, jnp.int32)
        key_ref[:, pl.ds(k0, tk)] = jnp.where(bits < 0, bits ^ INT_MAX, bits)
        return carry

    lax.fori_loop(0, n_kt, score_tile, 0)

    def count(pred):
        def body(kt, acc):
            k0 = pl.multiple_of(kt * tk, tk)
            hit = jnp.where(pred(key_ref[:, pl.ds(k0, tk)], k0 + lane_k), 1, 0)
            for j in range(tk // LANES):
                acc = acc + hit[:, j * LANES:(j + 1) * LANES]
            return acc
        acc = lax.fori_loop(0, n_kt, body, jnp.zeros((nq, LANES), jnp.int32))
        return jnp.sum(acc, axis=-1, keepdims=True)

    thr_ref[...] = jnp.full((nq, 1), NEG_INF_KEY + 1, jnp.int32)
    tie_ref[...] = jnp.full((nq, 1), INT_MAX, jnp.int32)

    @pl.when(q0 + nq > topk)
    def _():
        c0 = count(lambda kk, idx: kk >= 0)
        base0 = jnp.where(c0 >= topk, 0, INT_MIN)

        def bit_step(j, base):
            cand = base | (jnp.int32(1) << (30 - j))
            c = count(lambda kk, idx: kk >= cand)
            return jnp.where(c >= topk, cand, base)

        thr = lax.fori_loop(0, 31, bit_step, base0)
        n_gt = count(lambda kk, idx: kk > thr)
        n_eq = count(lambda kk, idx: kk == thr)
        need = topk - n_gt
        narrow = (q0 + 1 + lax.broadcasted_iota(jnp.int32, (nq, 1), 0)) > topk
        thr_ref[...] = jnp.where(narrow, thr, NEG_INF_KEY + 1)
        excess = jnp.where(narrow & (n_eq > need), 1, 0)

        @pl.when(jnp.max(excess) > 0)
        def _():
            def idx_step(j, x):
                cand = x | (jnp.int32(1) << (idx_bits - 1 - j))
                c = count(lambda kk, idx: (kk == thr) & (idx < cand))
                return jnp.where(c < need, cand, x)

            x = lax.fori_loop(0, idx_bits, idx_step, jnp.zeros((nq, 1), jnp.int32))
            tie_ref[...] = jnp.where(excess > 0, x, INT_MAX)

    m_ref[...] = jnp.full(m_ref.shape, neg, F32)
    l_ref[...] = jnp.zeros(l_ref.shape, F32)
    acc_ref[...] = jnp.zeros(acc_ref.shape, F32)
    thr = thr_ref[...]
    tie = tie_ref[...]
    scale = B_HEAD_DIM ** -0.5
    group = B_HEADS // B_KV_HEADS

    q_groups = [
        jnp.concatenate([q_ref[:, (g * group + hq) * B_HEAD_DIM:(g * group + hq + 1) * B_HEAD_DIM]
                         for hq in range(group)], axis=0)
        for g in range(B_KV_HEADS)]

    def attend_tile(kt, carry):
        k0 = pl.multiple_of(kt * tk, tk)
        keys = key_ref[:, pl.ds(k0, tk)]
        sel = (keys > thr) | ((keys == thr) & (k0 + lane_k <= tie))
        bias1 = jnp.where(sel, 0.0, neg)
        bias = jnp.concatenate([bias1] * group, axis=0)
        new = []
        for g in range(B_KV_HEADS):
            k_t = k_ref[pl.ds(k0, tk), g * B_HEAD_DIM:(g + 1) * B_HEAD_DIM]
            v_t = v_ref[pl.ds(k0, tk), g * B_HEAD_DIM:(g + 1) * B_HEAD_DIM]
            s = lax.dot_general(q_groups[g], k_t, (((1,), (1,)), ((), ())),
                                preferred_element_type=F32) * scale + bias
            m_old = m_ref[g]
            m_new = jnp.maximum(m_old, jnp.max(s, axis=-1, keepdims=True))
            m_safe = jnp.where(m_new == neg, 0.0, m_new)
            alpha = jnp.exp(m_old - m_safe)
            p = jnp.exp(s - m_safe)
            new.append((m_new,
                        alpha * l_ref[g] + jnp.sum(p, axis=-1, keepdims=True),
                        alpha * acc_ref[g] + jnp.dot(p.astype(BF16), v_t,
                                                     preferred_element_type=F32)))
        for g, (m_new, l_new, acc_new) in enumerate(new):
            m_ref[g] = m_new
            l_ref[g] = l_new
            acc_ref[g] = acc_new
        return carry

    lax.fori_loop(0, n_kt, attend_tile, 0)
    for g in range(B_KV_HEADS):
        out_g = acc_ref[g] / l_ref[g]
        for hq in range(group):
            h = g * group + hq
            o_ref[:, h * B_HEAD_DIM:(h + 1) * B_HEAD_DIM] = (
                out_g[hq * nq:(hq + 1) * nq]).astype(o_ref.dtype)


def dsa_attend(qi, w, q, ki, k, v, batch, seq):
    nqb = seq // Q_BLOCK
    topk = min(MAX_TOPK, seq // 4)
    idx_bits = max(1, (seq - 1).bit_length())
    rows_g = (B_HEADS // B_KV_HEADS) * Q_BLOCK
    qblk =lambda wd: pl.BlockSpec((Q_BLOCK, wd), lambda b, i: (b * nqb + i, 0))
    full = lambda wd: pl.BlockSpec((None, seq, wd), lambda b, i: (b, 0, 0))
    return pl.pallas_call(
        functools.partial(_dsa_kernel, topk=topk, idx_bits=idx_bits),
        grid=(batch, nqb),
        in_specs=[qblk(IDX_WIDTH), qblk(LANES), qblk(B_WIDTH),
                  full(IDX_DIM), full(B_KV_WIDTH), full(B_KV_WIDTH)],
        out_specs=qblk(B_WIDTH),
        out_shape=jax.ShapeDtypeStruct((batch * seq, B_WIDTH), BF16),
        scratch_shapes=[pltpu.VMEM((Q_BLOCK, seq), jnp.int32),
                        pltpu.VMEM((Q_BLOCK, 1), jnp.int32),
                        pltpu.VMEM((Q_BLOCK, 1), jnp.int32),
                        pltpu.VMEM((B_KV_HEADS, rows_g, 1), F32),
                        pltpu.VMEM((B_KV_HEADS, rows_g, 1), F32),
                        pltpu.VMEM((B_KV_HEADS, rows_g, B_HEAD_DIM), F32)],
        compiler_params=_cparams("parallel", "arbitrary"),
        name="dsa_attend",
    )(qi, w, q, ki, k, v)


def _pack_w_in(w):
    o_q = A_COLS
    o_k = o_q + B_WIDTH
    o_v = o_k + B_KV_WIDTH
    o_qi = o_v + B_KV_WIDTH
    o_kw = o_qi + IDX_WIDTH
    parts = [w[:, :3 * A_WIDTH], w[:, o_q:o_k], w[:, o_qi:o_kw], w[:, o_k:o_v], w[:, o_v:o_qi],
             w[:, 3 * A_WIDTH:A_COLS], w[:, o_kw:]]
    used = sum(p.shape[1] for p in parts)
    parts.append(jnp.zeros((w.shape[0], Z_COLS - used), w.dtype))
    return jnp.concatenate(parts, axis=1).astype(BF16)


def _pad_rows(w, r0, rows):
    return jnp.pad(w, ((r0, rows - r0 - w.shape[0]), (0, 0))).astype(BF16)


def kernel(x, c, positions, mod_w, mod_b, mod_table, final_table, norm_g, final_g,
           ev_w_in, ev_mu, ev_w0, ev_w2, ev_a0, ev_a2, ev_g2, ev_k_k, ev_k_a, ev_r_k,
           ev_lnx_g, ev_lnx_b, ev_w_out, ffn_w1, ffn_w3, ffn_w2,
           od_conv_in, od_conv_w, od_conv_out, moe_router, moe_w1, moe_w3, moe_w2):
    bsz, seq, d = x.shape
    t = bsz * seq
    row = lambda v: v.reshape(1, -1)
    per_batch = lambda v: v[:, None, :]

    c8 = jnp.pad(c, ((0, SUBLANES - bsz), (0, 0)))
    t0 = cond_proj(c8, mod_w, row(mod_b))[:bsz].reshape(bsz, N_MOD, d)
    tables = rope_tables(positions)
    xf = x.reshape(t, d)

    for i in range(DEPTH):
        j = i // 2
        mod = t0 + mod_table[i][None]
        mul1 = per_batch(norm_g[i, 0][None] * (1.0 + mod[:, 1]))
        add1 = per_batch(mod[:, 0])
        gate1 = per_batch(mod[:, 2])
        mul2 = per_batch(norm_g[i, 1][None] * (1.0 + mod[:, 4]))
        add2 = per_batch(mod[:, 3])
        gate2 = per_batch(mod[:, 5])

        if i % 2 == 0:
            z = norm_matmul(xf, mul1, add1, _pack_w_in(ev_w_in[j]), seq)
            mu = ev_mu[j]
            aw = A_WIDTH
            r, k, v, ld, a, g = rwkv_prep(
                z, row(mu[:aw]), row(mu[aw:2 * aw]), row(mu[2 * aw:3 * aw]), row(mu[3 * aw:]),
                _pad_rows(ev_w2[j], 0, LORA_COLS), _pad_rows(ev_a2[j], DECAY_LORA, LORA_COLS),
                _pad_rows(ev_g2[j], DECAY_LORA + ICLR_LORA, LORA_COLS),
                row(ev_w0[j]), row(ev_a0[j]), seq)
            b3 = lambda u: u.reshape(bsz, seq, -1)
            ya = rwkv_mix(b3(r), b3(k), b3(v), b3(ld), b3(a), b3(g),
                          row(ev_k_k[j]), row(ev_k_a[j]), row(ev_r_k[j]),
                          row(ev_lnx_g[j]), row(ev_lnx_b[j]))
            q, qi, kb, vb, ki, wi = dsa_prep(z, tables, seq)
            yb = dsa_attend(qi, wi, q, b3(ki), b3(kb), b3(vb), bsz, seq)
            m = jnp.concatenate([ya.reshape(t, aw), yb], axis=1)
            xf = matmul_residual(m, ev_w_out[j].astype(BF16), xf, gate1, seq)
            gf = norm_swiglu(xf, mul2, add2, ffn_w1[j].astype(BF16), ffn_w3[j].astype(BF16), seq)
            xf = matmul_residual(gf, ffn_w2[j].astype(BF16), xf, gate2, seq)
        else:
            bcx = norm_matmul(xf, mul1, add1, od_conv_in[j].astype(BF16), seq)
            cw = jnp.pad(od_conv_w[j], ((0, SUBLANES - CONV_WIDTH), (0, 0)))
            cm = gated_short_conv(bcx, cw, seq)
            xf = matmul_residual(cm, od_conv_out[j].astype(BF16), xf, gate1, seq)
            wr = jnp.pad(moe_router[j], ((0, 0), (0, LANES - N_EXPERTS))).astype(BF16)
            xf = moe_block(xf, mul2, add2, wr, moe_w1[j].astype(BF16), moe_w3[j].astype(BF16),
                           moe_w2[j].astype(BF16), gate2, seq)

    mul_f = per_batch(final_g[None] * (1.0 + t0[:, 1] + final_table[1][None]))
    add_f = per_batch(t0[:, 0] + final_table[0][None])
    return final_norm(xf, mul_f, add_f, seq).reshape(bsz, seq, d)
```
